```python
import jax, jax.numpy as jnp
from jax import lax
import numpy as np

D_MODEL = 1024
BATCH = 8
SEQ = 4096
DEPTH = 1

HEAD_DIM = 64
RET_HEADS = 8
SWA_Q_HEADS = 8
SWA_KV_HEADS = 2
SWA_GROUP = SWA_Q_HEADS // SWA_KV_HEADS
RET_WIDTH = RET_HEADS * HEAD_DIM
SWA_Q_WIDTH = SWA_Q_HEADS * HEAD_DIM
SWA_KV_WIDTH = SWA_KV_HEADS * HEAD_DIM
MIX_WIDTH = RET_WIDTH + SWA_Q_WIDTH
IN_SPLITS = (RET_WIDTH, RET_WIDTH, RET_WIDTH, RET_WIDTH, SWA_Q_WIDTH, SWA_KV_WIDTH, SWA_KV_WIDTH)
IN_WIDTH = sum(IN_SPLITS)
RET_CHUNK = 128
ROPE_BASE = 10000.0
WINDOW = 128
SWA_BLOCK = 128
PEER_HEADS = 8
PEER_KEYS = 128
PEER_EXPERTS = PEER_KEYS * PEER_KEYS
PEER_QDIM = 256
PEER_HALF = PEER_QDIM // 2
PEER_TOPK = 16
PEER_TOKEN_BLOCK = 128
PLE_DIM = 256
EPS = 1e-6

kernel_name = "hybrid_retention_swa_peer_block"


def rmsnorm(x, g):
    xf = x.astype(jnp.float32)
    y = xf * lax.rsqrt(jnp.mean(xf * xf, axis=-1, keepdims=True) + EPS)
    return (y * g.astype(jnp.float32)).astype(x.dtype)


def head_layernorm(x):
    xf = x.astype(jnp.float32)
    mu = jnp.mean(xf, axis=-1, keepdims=True)
    xc = xf - mu
    var = jnp.mean(xc * xc, axis=-1, keepdims=True)
    return (xc * lax.rsqrt(var + EPS)).astype(x.dtype)


def rotary(x, pos):
    d = x.shape[-1]
    half = d // 2
    freqs = ROPE_BASE ** (-jnp.arange(half, dtype=jnp.float32) / half)
    ang = pos[:, None] * freqs[None, :]
    c = jnp.cos(ang)[None, :, None, :]
    s = jnp.sin(ang)[None, :, None, :]
    x1, x2 = x[..., :half], x[..., half:]
    out = jnp.concatenate([x1 * c - x2 * s, x1 * s + x2 * c], axis=-1)
    return out.astype(x.dtype)


def retention_chunkwise(q, k, v):
    B, S, H, d = q.shape
    C = RET_CHUNK
    NC = S // C
    gamma = 1.0 - jnp.exp2(-5.0 - jnp.arange(H, dtype=jnp.float32))
    log_g = jnp.log(gamma)
    k = k * (d ** -0.5)
    qc = q.reshape(B, NC, C, H, d)
    kc = k.reshape(B, NC, C, H, d)
    vc = v.reshape(B, NC, C, H, d)
    idx = jnp.arange(C)
    rel = idx[:, None] - idx[None, :]
    dec = jnp.where(rel[None] >= 0,
                    jnp.exp(log_g[:, None, None] * jnp.maximum(rel, 0)[None].astype(jnp.float32)),
                    0.0)
    att = jnp.einsum('bnihd,bnjhd->bnhij', qc, kc) * dec[None, None]
    o_intra = jnp.einsum('bnhij,bnjhe->bnihe', att, vc)
    w_k = jnp.exp(log_g[None, :] * (C - 1 - idx)[:, None].astype(jnp.float32))
    kv = jnp.einsum('bnjhd,jh,bnjhe->bnhde', kc, w_k, vc)
    chunk_decay = jnp.exp(log_g * C)

    def step(state, kv_n):
        return state * chunk_decay[None, :, None, None] + kv_n, state

    init = jnp.zeros((B, H, d, d), dtype=kv.dtype)
    _, prev = lax.scan(step, init, jnp.moveaxis(kv, 1, 0))
    prev = jnp.moveaxis(prev, 0, 1)
    w_q = jnp.exp(log_g[None, :] * (idx + 1)[:, None].astype(jnp.float32))
    o_cross = jnp.einsum('bnihd,bnhde->bnihe', qc, prev) * w_q[None, None, :, :, None]
    return (o_intra + o_cross).reshape(B, S, H, d)


def sliding_window_sink_attention(q, k, v, sinks):
    B, S, Hq, d = q.shape
    L = SWA_BLOCK
    NB = S // L
    qb = q.reshape(B, NB, L, SWA_KV_HEADS, SWA_GROUP, d)
    kb = k.reshape(B, NB, L, SWA_KV_HEADS, d)
    vb = v.reshape(B, NB, L, SWA_KV_HEADS, d)
    pad = ((0, 0), (1, 0), (0, 0), (0, 0), (0, 0))
    kcat = jnp.concatenate([jnp.pad(kb, pad)[:, :-1], kb], axis=2)
    vcat = jnp.concatenate([jnp.pad(vb, pad)[:, :-1], vb], axis=2)
    s = jnp.einsum('bnqhgd,bnkhd->bnhgqk', qb, kcat).astype(jnp.float32) * (d ** -0.5)
    qi = jnp.arange(L)[:, None]
    ki = jnp.arange(2 * L)[None, :]
    rel = qi + L - ki
    band = (rel >= 0) & (rel < WINDOW)
    mask = band[None] & ((jnp.arange(NB) > 0)[:, None, None] | (ki >= L)[None])
    s = jnp.where(mask[None, :, None, None], s, -jnp.inf)
    sink = sinks.astype(jnp.float32).reshape(SWA_KV_HEADS, SWA_GROUP)[None, None, :, :, None, None]
    m = jnp.maximum(jnp.max(s, axis=-1, keepdims=True), sink)
    e = jnp.exp(s - m)
    pr = e / (jnp.sum(e, axis=-1, keepdims=True) + jnp.exp(sink - m))
    o = jnp.einsum('bnhgqk,bnkhd->bnqhgd', pr.astype(v.dtype), vcat)
    return o.reshape(B, S, Hq * d)


def peer_ffn(h, w_query, sub_keys, emb_u, emb_v):
    B, S, D = h.shape
    q = (h @ w_query).reshape(B, S, PEER_HEADS, 2, PEER_HALF)
    sc = jnp.einsum('bshpc,pkc->bshpk', q, sub_keys).astype(jnp.float32)
    top_s, top_i = lax.top_k(sc, PEER_TOPK)
    cand_s = top_s[..., 0, :, None] + top_s[..., 1, None, :]
    cand_i = top_i[..., 0, :, None] * PEER_KEYS + top_i[..., 1, None, :]
    cand_s = cand_s.reshape(B, S, PEER_HEADS, PEER_TOPK * PEER_TOPK)
    cand_i = cand_i.reshape(B, S, PEER_HEADS, PEER_TOPK * PEER_TOPK)
    best_s, pos = lax.top_k(cand_s, PEER_TOPK)
    expert = jnp.take_along_axis(cand_i, pos, axis=-1)
    gate = jax.nn.softmax(best_s, axis=-1).astype(h.dtype)
    T = B * S
    nblk = T // PEER_TOKEN_BLOCK
    hb = h.reshape(nblk, PEER_TOKEN_BLOCK, D)
    eb = expert.reshape(nblk, PEER_TOKEN_BLOCK, PEER_HEADS, PEER_TOPK)
    gb = gate.reshape(nblk, PEER_TOKEN_BLOCK, PEER_HEADS, PEER_TOPK)

    def block(args):
        hx, ex, gx = args
        u = emb_u[ex]
        a = jnp.einsum('td,thkd->thk', hx, u)
        act = jax.nn.gelu(a, approximate=False) * gx
        vv = emb_v[ex]
        return jnp.einsum('thk,thkd->td', act, vv)

    y = lax.map(block, (hb, eb, gb))
    return y.reshape(B, S, D)


def setup_inputs(seed: int = 0) -> dict:
    key = jax.random.key(seed)
    ks = jax.random.split(key, 16)
    f32 = jnp.float32
    nrm = lambda k, shape, scale: jax.random.normal(k, shape, f32) * scale
    return {
        "x": nrm(ks[0], (BATCH, SEQ, D_MODEL), 1.0),
        "p": nrm(ks[1], (DEPTH, BATCH, SEQ, PLE_DIM), 1.0),
        "g_mix": 1.0 + nrm(ks[2], (DEPTH, D_MODEL), 0.02),
        "w_in": nrm(ks[3], (DEPTH, D_MODEL, IN_WIDTH), D_MODEL ** -0.5),
        "q_norm": 1.0 + nrm(ks[4], (DEPTH, HEAD_DIM), 0.02),
        "k_norm": 1.0 + nrm(ks[5], (DEPTH, HEAD_DIM), 0.02),
        "sinks": nrm(ks[6], (DEPTH, SWA_Q_HEADS), 0.5),
        "w_out": nrm(ks[7], (DEPTH, MIX_WIDTH, D_MODEL), MIX_WIDTH ** -0.5),
        "g_ffn": 1.0 + nrm(ks[8], (DEPTH, D_MODEL), 0.02),
        "peer_w_query": nrm(ks[9], (DEPTH, D_MODEL, PEER_HEADS * PEER_QDIM), D_MODEL ** -0.5),
        "peer_sub_keys": nrm(ks[10], (DEPTH, 2, PEER_KEYS, PEER_HALF), PEER_HALF ** -0.5),
        "peer_u": nrm(ks[11], (DEPTH, PEER_EXPERTS, D_MODEL), D_MODEL ** -0.5),
        "peer_v": nrm(ks[12], (DEPTH, PEER_EXPERTS, D_MODEL), 0.3),
        "g_ple": 1.0 + nrm(ks[13], (DEPTH, D_MODEL), 0.02),
        "w_ple_gate": nrm(ks[14], (DEPTH, D_MODEL, D_MODEL), D_MODEL ** -0.5),
        "w_ple_proj": nrm(ks[15], (DEPTH, PLE_DIM, D_MODEL), PLE_DIM ** -0.5),
    }


def reference(x, p, g_mix, w_in, q_norm, k_norm, sinks, w_out, g_ffn,
              peer_w_query, peer_sub_keys, peer_u, peer_v, g_ple, w_ple_gate, w_ple_proj):
    B, S, _ = x.shape
    pos = jnp.arange(S, dtype=jnp.float32)
    split_pts = list(np.cumsum(IN_SPLITS)[:-1])
    for i in range(DEPTH):
        h = rmsnorm(x, g_mix[i])
        proj = h @ w_in[i]
        rq, rk, rv, rg, sq, sk, sv = jnp.split(proj, split_pts, axis=-1)
        rq = rotary(rq.reshape(B, S, RET_HEADS, HEAD_DIM), pos)
        rk = rotary(rk.reshape(B, S, RET_HEADS, HEAD_DIM), pos)
        rv = rv.reshape(B, S, RET_HEADS, HEAD_DIM)
        ro = head_layernorm(retention_chunkwise(rq, rk, rv)).reshape(B, S, RET_WIDTH)
        ro = jax.nn.silu(rg) * ro.astype(rg.dtype)
        sq = rmsnorm(sq.reshape(B, S, SWA_Q_HEADS, HEAD_DIM), q_norm[i])
        sk = rmsnorm(sk.reshape(B, S, SWA_KV_HEADS, HEAD_DIM), k_norm[i])
        sv = sv.reshape(B, S, SWA_KV_HEADS, HEAD_DIM)
        so = sliding_window_sink_attention(sq, sk, sv, sinks[i])
        mixed = jnp.concatenate([ro, so.astype(ro.dtype)], axis=-1)
        x = x + mixed @ w_out[i]
        h = rmsnorm(x, g_ffn[i])
        x = x + peer_ffn(h, peer_w_query[i], peer_sub_keys[i], peer_u[i], peer_v[i]).astype(x.dtype)
        gate = jax.nn.sigmoid(rmsnorm(x, g_ple[i]) @ w_ple_gate[i])
        x = x + gate * (p[i] @ w_ple_proj[i])
    return x
```

```python
import functools

import jax
import jax.numpy as jnp
from jax import lax
from jax.experimental import pallas as pl
from jax.experimental.pallas import tpu as pltpu

F32 = jnp.float32
BF16 = jnp.bfloat16

HEAD_DIM = 64
RET_HEADS = 8
SWA_Q_HEADS = 8
SWA_KV_HEADS = 2
SWA_GROUP = SWA_Q_HEADS // SWA_KV_HEADS
RET_WIDTH = RET_HEADS * HEAD_DIM
SWA_Q_WIDTH = SWA_Q_HEADS * HEAD_DIM
SWA_KV_WIDTH = SWA_KV_HEADS * HEAD_DIM
CHUNK = 128
ROPE_BASE = 10000.0
PEER_HEADS = 8
PEER_KEYS = 128
PEER_HALF = 128
PEER_TOPK = 16
PEER_PAIRS = PEER_HEADS * PEER_TOPK
EPS = 1e-6

LANES = 128
F32_SUBLANES = 8
BF16_SUBLANES = 16
VMEM_LIMIT_DENSE = 48 * 1024 * 1024
VMEM_LIMIT_TABLE = 56 * 1024 * 1024

ROWS_PER_EXPERT = 1024 // LANES
TILE_ROWS = BF16_SUBLANES


def _rms(x, g):
    ms = jnp.mean(x * x, axis=-1, keepdims=True)
    return x * lax.rsqrt(ms + EPS) * g


def _dot(a, b):
    return jnp.dot(a, b, preferred_element_type=F32)


def _dot_nt(a, b):
    return lax.dot_general(a, b, (((1,), (1,)), ((), ())), preferred_element_type=F32)


def _dot_tn(a, b):
    return lax.dot_general(a, b, (((0,), (0,)), ((), ())), preferred_element_type=F32)


def _in_proj_kernel(x_ref, g_ref, w_ref, o_ref):
    h = _rms(x_ref[...], g_ref[...])
    o_ref[...] = _dot(h.astype(BF16), w_ref[...])


def _in_proj(x2d, g, w, tm):
    t, d = x2d.shape
    n = w.shape[1]
    return pl.pallas_call(
        _in_proj_kernel,
        grid=(t // tm,),
        in_specs=[
            pl.BlockSpec((tm, d), lambda i: (i, 0)),
            pl.BlockSpec((1, d), lambda i: (0, 0)),
            pl.BlockSpec((d, n), lambda i: (0, 0)),
        ],
        out_specs=pl.BlockSpec((tm, n), lambda i: (i, 0)),
        out_shape=jax.ShapeDtypeStruct((t, n), F32),
        compiler_params=pltpu.CompilerParams(
            dimension_semantics=("parallel",), vmem_limit_bytes=VMEM_LIMIT_DENSE),
        name="in_proj",
    )(x2d, g, w)


def _retention_kernel(cd_ref, q_ref, k_ref, v_ref, g_ref, cos_ref, sin_ref, dec_ref, wk_ref, wq_ref,
                      o_ref, state_ref):
    n = pl.program_id(1)

    @pl.when(n == 0)
    def _():
        state_ref[...] = jnp.zeros_like(state_ref)

    c = cos_ref[...]
    s = sin_ref[...]
    lane = lax.broadcasted_iota(jnp.int32, (CHUNK, RET_WIDTH), 1)
    first_half = (lane % HEAD_DIM) < (HEAD_DIM // 2)

    def rot(x):
        partner = jnp.where(first_half,
                            pltpu.roll(x, RET_WIDTH - HEAD_DIM // 2, 1),
                            pltpu.roll(x, HEAD_DIM // 2, 1))
        return x * c + partner * s

    q = rot(q_ref[...])
    k = rot(k_ref[...]) * (HEAD_DIM ** -0.5)
    kw = k * wk_ref[...]
    v = v_ref[...]
    gate = g_ref[...]
    wq = wq_ref[...]
    for h in range(RET_HEADS):
        sl = slice(h * HEAD_DIM, (h + 1) * HEAD_DIM)
        qh = q[:, sl].astype(BF16)
        kh = k[:, sl].astype(BF16)
        vh = v[:, sl].astype(BF16)
        att = _dot_nt(qh, kh) * dec_ref[h]
        o = _dot(att.astype(BF16), vh)
        st = state_ref[h]
        o = o + _dot(qh, st.astype(BF16)) * wq[:, sl]
        state_ref[h] = st * cd_ref[h] + _dot_tn(kw[:, sl].astype(BF16), vh)
        mu = jnp.mean(o, axis=-1, keepdims=True)
        oc = o - mu
        var = jnp.mean(oc * oc, axis=-1, keepdims=True)
        on = oc * lax.rsqrt(var + EPS)
        gh = gate[:, sl]
        o_ref[:, sl] = gh * jax.nn.sigmoid(gh) * on


def _retention(proj3, tables):
    b, s, _ = proj3.shape
    cd, cos_t, sin_t, dec, wk, wq = tables
    col = lambda j: pl.BlockSpec((None, CHUNK, RET_WIDTH), lambda bi, ni, j=j: (bi, ni, j))
    pos_spec = pl.BlockSpec((CHUNK, RET_WIDTH), lambda bi, ni: (ni, 0))
    const2 = pl.BlockSpec((CHUNK, RET_WIDTH), lambda bi, ni: (0, 0))
    return pl.pallas_call(
        _retention_kernel,
        grid=(b, s // CHUNK),
        in_specs=[
            pl.BlockSpec(memory_space=pltpu.SMEM),
            col(0), col(1), col(2), col(3),
            pos_spec, pos_spec,
            pl.BlockSpec((RET_HEADS, CHUNK, CHUNK), lambda bi, ni: (0, 0, 0)),
            const2, const2,
        ],
        out_specs=pl.BlockSpec((None, CHUNK, RET_WIDTH), lambda bi, ni: (bi, ni, 0)),
        out_shape=jax.ShapeDtypeStruct((b, s, RET_WIDTH), F32),
        scratch_shapes=[pltpu.VMEM((RET_HEADS, HEAD_DIM, HEAD_DIM), F32)],
        compiler_params=pltpu.CompilerParams(
            dimension_semantics=("parallel", "arbitrary"), vmem_limit_bytes=VMEM_LIMIT_DENSE),
        name="retention",
    )(cd, proj3, proj3, proj3, proj3, cos_t, sin_t, dec, wk, wq)


def _retention_tables(s):
    half = HEAD_DIM // 2
    pos = jnp.arange(s, dtype=F32)
    freqs = ROPE_BASE ** (-jnp.arange(half, dtype=F32) / half)
    ang = pos[:, None] * freqs[None, :]
    c = jnp.cos(ang)
    sn = jnp.sin(ang)
    cos_t = jnp.tile(jnp.concatenate([c, c], axis=-1), (1, RET_HEADS))
    sin_t = jnp.tile(jnp.concatenate([-sn, sn], axis=-1), (1, RET_HEADS))
    gamma = 1.0 - jnp.exp2(-5.0 - jnp.arange(RET_HEADS, dtype=F32))
    log_g = jnp.log(gamma)
    idx = jnp.arange(CHUNK)
    rel = idx[:, None] - idx[None, :]
    dec = jnp.where(rel[None] >= 0,
                    jnp.exp(log_g[:, None, None] * jnp.maximum(rel, 0)[None].astype(F32)),
                    0.0)
    w_k = jnp.exp(log_g[None, :] * (CHUNK - 1 - idx)[:, None].astype(F32))
    w_q = jnp.exp(log_g[None, :] * (idx + 1)[:, None].astype(F32))
    cd = jnp.exp(log_g * CHUNK)
    wk = jnp.repeat(w_k, HEAD_DIM, axis=1)
    wq = jnp.repeat(w_q, HEAD_DIM, axis=1)
    return cd, cos_t, sin_t, dec, wk, wq


def _swa_kernel(sink_ref, q_ref, kc_ref, vc_ref, kp_ref, vp_ref, qn_ref, kn_ref, o_ref):
    n = pl.program_id(1)
    qi = lax.broadcasted_iota(jnp.int32, (CHUNK, 2 * CHUNK), 0)
    ki = lax.broadcasted_iota(jnp.int32, (CHUNK, 2 * CHUNK), 1)
    rel = qi + CHUNK - ki
    mask = (rel >= 0) & (rel < CHUNK) & ((n > 0) | (ki >= CHUNK))
    q = q_ref[...]
    qn = qn_ref[...]
    kn = kn_ref[...]
    for g in range(SWA_KV_HEADS):
        sl = slice(g * HEAD_DIM, (g + 1) * HEAD_DIM)
        kcat = jnp.concatenate([kp_ref[:, sl], kc_ref[:, sl]], axis=0)
        kcat = _rms(kcat, kn).astype(BF16)
        vcat = jnp.concatenate([vp_ref[:, sl], vc_ref[:, sl]], axis=0).astype(BF16)
        for j in range(SWA_GROUP):
            h = g * SWA_GROUP + j
            hs = slice(h * HEAD_DIM, (h + 1) * HEAD_DIM)
            qh = _rms(q[:, hs], qn).astype(BF16)
            sc = _dot_nt(qh, kcat) * (HEAD_DIM ** -0.5)
            sc = jnp.where(mask, sc, -jnp.inf)
            sink = sink_ref[h]
            m = jnp.maximum(jnp.max(sc, axis=-1, keepdims=True), sink)
            e = jnp.exp(sc - m)
            pr = e / (jnp.sum(e, axis=-1, keepdims=True) + jnp.exp(sink - m))
            o_ref[:, hs] = _dot(pr.astype(BF16), vcat)


def _swa(proj3, sinks, qn, kn):
    b, s, _ = proj3.shape
    q_blk = (4 * RET_WIDTH) // SWA_Q_WIDTH
    k_blk = (4 * RET_WIDTH + SWA_Q_WIDTH) // SWA_KV_WIDTH
    v_blk = k_blk + 1
    cur = lambda j: pl.BlockSpec((None, CHUNK, SWA_KV_WIDTH), lambda bi, ni, j=j: (bi, ni, j))
    prev = lambda j: pl.BlockSpec((None, CHUNK, SWA_KV_WIDTH),
                                  lambda bi, ni, j=j: (bi, jnp.maximum(ni - 1, 0), j))
    return pl.pallas_call(
        _swa_kernel,
        grid=(b, s // CHUNK),
        in_specs=[
            pl.BlockSpec(memory_space=pltpu.SMEM),
            pl.BlockSpec((None, CHUNK, SWA_Q_WIDTH), lambda bi, ni: (bi, ni, q_blk)),
            cur(k_blk), cur(v_blk), prev(k_blk), prev(v_blk),
            pl.BlockSpec((1, HEAD_DIM), lambda bi, ni: (0, 0)),
            pl.BlockSpec((1, HEAD_DIM), lambda bi, ni: (0, 0)),
        ],
        out_specs=pl.BlockSpec((None, CHUNK, SWA_Q_WIDTH), lambda bi, ni: (bi, ni, 0)),
        out_shape=jax.ShapeDtypeStruct((b, s, SWA_Q_WIDTH), F32),
        compiler_params=pltpu.CompilerParams(
            dimension_semantics=("parallel", "parallel"), vmem_limit_bytes=VMEM_LIMIT_DENSE),
        name="swa",
    )(sinks, proj3, proj3, proj3, proj3, proj3, qn, kn)


def _out_proj_kernel(x_ref, ro_ref, so_ref, wo1_ref, wo2_ref, g_ref, wq_ref, x1_ref, h2_ref, qp_ref):
    x1 = (x_ref[...] + _dot(ro_ref[...].astype(BF16), wo1_ref[...])
          + _dot(so_ref[...].astype(BF16), wo2_ref[...]))
    x1_ref[...] = x1
    h2 = _rms(x1, g_ref[...])
    h2_ref[...] = h2
    qp = _dot(h2.astype(BF16), wq_ref[...])
    width = qp_ref.shape[2]
    for h in range(PEER_HEADS):
        qp_ref[h] = qp[:, h * width:(h + 1) * width]


def _out_proj(x2d, ro, so, wo1, wo2, g, wq, tm):
    t, d = x2d.shape
    nq = wq.shape[1]
    width = nq // PEER_HEADS
    row = lambda w: pl.BlockSpec((tm, w), lambda i: (i, 0))
    full = lambda a: pl.BlockSpec(a.shape, lambda i: (0,) * a.ndim)
    return pl.pallas_call(
        _out_proj_kernel,
        grid=(t // tm,),
        in_specs=[row(d), row(ro.shape[1]), row(so.shape[1]), full(wo1), full(wo2), full(g), full(wq)],
        out_specs=[row(d), row(d), pl.BlockSpec((PEER_HEADS, tm, width), lambda i: (0, i, 0))],
        out_shape=[jax.ShapeDtypeStruct((t, d), F32), jax.ShapeDtypeStruct((t, d), F32),
                   jax.ShapeDtypeStruct((PEER_HEADS, t, width), F32)],
        compiler_params=pltpu.CompilerParams(
            dimension_semantics=("parallel",), vmem_limit_bytes=VMEM_LIMIT_DENSE),
        name="out_proj",
    )(x2d, ro, so, wo1, wo2, g, wq)


def _peer_topk_kernel(qp_ref, keys_ref, e0_ref, e1_ref, idx_ref, gate_ref):
    h = pl.program_id(1)
    tm = qp_ref.shape[0]
    neg = -jnp.inf
    lane = lax.broadcasted_iota(jnp.int32, (tm, LANES), 1).astype(F32)
    top_s = []
    top_i = []
    for p in range(2):
        qhp = qp_ref[:, p * PEER_HALF:(p + 1) * PEER_HALF].astype(BF16)
        sc = _dot_nt(qhp, keys_ref[p])
        ts = jnp.zeros((tm, LANES), F32)
        ti = jnp.zeros((tm, LANES), F32)
        for a in range(PEER_TOPK):
            m = jnp.max(sc, axis=-1, keepdims=True)
            i = jnp.min(jnp.where(sc == m, lane, float(LANES)), axis=-1, keepdims=True)
            sc = jnp.where(lane == i, neg, sc)
            ts = jnp.where(lane == float(a), m, ts)
            ti = jnp.where(lane == float(a), i, ti)
        top_s.append(ts)
        top_i.append(ti)
    hi = lax.Precision.HIGHEST
    cs = (jnp.dot(top_s[0], e0_ref[...], precision=hi, preferred_element_type=F32)
          + jnp.dot(top_s[1], e1_ref[...], precision=hi, preferred_element_type=F32))
    ci = (jnp.dot(top_i[0], e0_ref[...], precision=hi, preferred_element_type=F32) * float(PEER_KEYS)
          + jnp.dot(top_i[1], e1_ref[...], precision=hi, preferred_element_type=F32))
    ncand = PEER_TOPK * PEER_TOPK
    lane2 = lax.broadcasted_iota(jnp.int32, (tm, ncand), 1).astype(F32)
    best = jnp.zeros((tm, LANES), F32)
    expert = jnp.zeros((tm, LANES), F32)
    slot = lax.broadcasted_iota(jnp.int32, (tm, LANES), 1) % PEER_TOPK
    for k in range(PEER_TOPK):
        m = jnp.max(cs, axis=-1, keepdims=True)
        pos = jnp.min(jnp.where(cs == m, lane2, float(ncand)), axis=-1, keepdims=True)
        sel = lane2 == pos
        e = jnp.max(jnp.where(sel, ci, -1.0), axis=-1, keepdims=True)
        cs = jnp.where(sel, neg, cs)
        best = jnp.where(slot == k, m, best)
        expert = jnp.where(slot == k, e, expert)
    top = jnp.max(best, axis=-1, keepdims=True)
    ex = jnp.exp(best - top)
    denom = jnp.sum(ex, axis=-1, keepdims=True) * (float(PEER_TOPK) / float(LANES))
    gate = ex / denom
    mine = (lax.broadcasted_iota(jnp.int32, (tm, LANES), 1) // PEER_TOPK) == h

    @pl.when(h == 0)
    def _():
        idx_ref[...] = jnp.zeros_like(idx_ref)
        gate_ref[...] = jnp.zeros_like(gate_ref)

    idx_ref[...] = jnp.where(mine, expert.astype(jnp.int32), idx_ref[...])
    gate_ref[...] = jnp.where(mine, gate, gate_ref[...])


def _peer_topk(qp, keys, tm):
    nh, t, width = qp.shape
    ncand = PEER_TOPK * PEER_TOPK
    r = jnp.arange(LANES)[:, None]
    cidx = jnp.arange(ncand)[None, :]
    e0 = ((cidx // PEER_TOPK == r) & (r < PEER_TOPK)).astype(F32)
    e1 = ((cidx % PEER_TOPK == r) & (r < PEER_TOPK)).astype(F32)
    full = lambda a: pl.BlockSpec(a.shape, lambda i, h: (0,) * a.ndim)
    out_spec = pl.BlockSpec((tm, LANES), lambda i, h: (i, 0))
    return pl.pallas_call(
        _peer_topk_kernel,
        grid=(t // tm, nh),
        in_specs=[pl.BlockSpec((None, tm, width), lambda i, h: (h, i, 0)), full(keys), full(e0), full(e1)],
        out_specs=[out_spec, out_spec],
        out_shape=[jax.ShapeDtypeStruct((t, PEER_PAIRS), jnp.int32),
                   jax.ShapeDtypeStruct((t, PEER_PAIRS), F32)],
        compiler_params=pltpu.CompilerParams(
            dimension_semantics=("parallel", "arbitrary"), vmem_limit_bytes=VMEM_LIMIT_DENSE),
        name="peer_topk",
    )(qp, keys, e0, e1)


def _tile_start(e):
    return pl.multiple_of((e >> 1) * TILE_ROWS, TILE_ROWS)


def _peer_u_kernel(idx_ref, h_ref, gate_ref, sel_ref, tab_ref, act_ref, hsel_ref, xs_ref):
    tb = h_ref.shape[0]
    ones = jnp.ones((BF16_SUBLANES, LANES), BF16)
    zeros8 = jnp.zeros((ROWS_PER_EXPERT, LANES), BF16)

    def token(t, carry):
        ht = h_ref[t].astype(BF16)
        hsel_ref[0] = jnp.concatenate([ht, zeros8], axis=0)
        hsel_ref[1] = jnp.concatenate([zeros8, ht], axis=0)
        for j in range(PEER_PAIRS):
            e = idx_ref[t, j]
            x = tab_ref[pl.ds(_tile_start(e), TILE_ROWS), :]
            xs_ref[pl.ds(j * TILE_ROWS, TILE_ROWS), :] = x * hsel_ref[e & 1]
        z = _dot(sel_ref[...], xs_ref[...])
        z_hi = z.astype(BF16)
        z_lo = (z - z_hi.astype(F32)).astype(BF16)
        a = _dot_nt(ones, z_hi) + _dot_nt(ones, z_lo)
        act_ref[pl.ds(t, 1), :] = a[0:1, :]
        return carry

    lax.fori_loop(0, tb, token, 0)
    a = act_ref[...]
    act_ref[...] = 0.5 * a * (1.0 + lax.erf(a * (2.0 ** -0.5))) * gate_ref[...]


def _peer_u(idx, h3, gate, tab, tb):
    t = idx.shape[0]
    j = jnp.arange(PEER_PAIRS)[:, None]
    c = jnp.arange(PEER_PAIRS * TILE_ROWS)[None, :]
    sel = (c // TILE_ROWS == j).astype(BF16)
    return pl.pallas_call(
        _peer_u_kernel,
        grid=(t // tb,),
        in_specs=[
            pl.BlockSpec((tb, PEER_PAIRS), lambda i: (i, 0), memory_space=pltpu.SMEM),
            pl.BlockSpec((tb, ROWS_PER_EXPERT, LANES), lambda i: (i, 0, 0)),
            pl.BlockSpec((tb, PEER_PAIRS), lambda i: (i, 0)),
            pl.BlockSpec(sel.shape, lambda i: (0, 0)),
            pl.BlockSpec(memory_space=pltpu.VMEM),
        ],
        out_specs=pl.BlockSpec((tb, PEER_PAIRS), lambda i: (i, 0)),
        out_shape=jax.ShapeDtypeStruct((t, PEER_PAIRS), F32),
        scratch_shapes=[pltpu.VMEM((2, TILE_ROWS, LANES), BF16),
                        pltpu.VMEM((PEER_PAIRS * TILE_ROWS, LANES), BF16)],
        compiler_params=pltpu.CompilerParams(
            dimension_semantics=("parallel",), vmem_limit_bytes=VMEM_LIMIT_TABLE),
        name="peer_u",
    )(idx, h3, gate, sel, tab)


def _peer_v_kernel(idx_ref, idxv_ref, act_ref, exp_ref, tab_ref, y_ref, xs_ref):
    tb = act_ref.shape[0]
    ncol = PEER_PAIRS * TILE_ROWS
    row = lax.broadcasted_iota(jnp.int32, (BF16_SUBLANES, ncol), 0)
    q = lax.broadcasted_iota(jnp.int32, (BF16_SUBLANES, ncol), 1) % TILE_ROWS
    hi = lax.Precision.HIGHEST

    def token(t, carry):
        for j in range(PEER_PAIRS):
            e = idx_ref[t, j]
            xs_ref[pl.ds(j * TILE_ROWS, TILE_ROWS), :] = tab_ref[pl.ds(_tile_start(e), TILE_ROWS), :]
        a_row = act_ref[pl.ds(t, 1), :]
        p_row = (idxv_ref[pl.ds(t, 1), :] & 1).astype(F32)
        a_exp = jnp.dot(a_row, exp_ref[...], precision=hi, preferred_element_type=F32)
        p_exp = jnp.dot(p_row, exp_ref[...], precision=hi, preferred_element_type=F32)
        want = p_exp.astype(jnp.int32) * ROWS_PER_EXPERT + row
        coef = jnp.where((q == want) & (row < ROWS_PER_EXPERT), a_exp, 0.0)
        c_hi = coef.astype(BF16)
        c_lo = (coef - c_hi.astype(F32)).astype(BF16)
        xs = xs_ref[...]
        y = _dot(c_hi, xs) + _dot(c_lo, xs)
        y_ref[t] = y[0:ROWS_PER_EXPERT, :]
        return carry

    lax.fori_loop(0, tb, token, 0)


def _peer_v(idx, act, tab, tb):
    t = idx.shape[0]
    j = jnp.arange(PEER_PAIRS)[:, None]
    c = jnp.arange(PEER_PAIRS * TILE_ROWS)[None, :]
    expand = (c // TILE_ROWS == j).astype(F32)
    return pl.pallas_call(
        _peer_v_kernel,
        grid=(t // tb,),
        in_specs=[
            pl.BlockSpec((tb, PEER_PAIRS), lambda i: (i, 0), memory_space=pltpu.SMEM),
            pl.BlockSpec((tb, PEER_PAIRS), lambda i: (i, 0)),
            pl.BlockSpec((tb, PEER_PAIRS), lambda i: (i, 0)),
            pl.BlockSpec(expand.shape, lambda i: (0, 0)),
            pl.BlockSpec(memory_space=pltpu.VMEM),
        ],
        out_specs=pl.BlockSpec((tb, ROWS_PER_EXPERT, LANES), lambda i: (i, 0, 0)),
        out_shape=jax.ShapeDtypeStruct((t, ROWS_PER_EXPERT, LANES), F32),
        scratch_shapes=[pltpu.VMEM((PEER_PAIRS * TILE_ROWS, LANES), BF16)],
        compiler_params=pltpu.CompilerParams(
            dimension_semantics=("parallel",), vmem_limit_bytes=VMEM_LIMIT_TABLE),
        name="peer_v",
    )(idx, idx, act, expand, tab)


def _ple_kernel(x1_ref, y_ref, p_ref, g_ref, wg_ref, wp_ref, o_ref):
    x2 = x1_ref[...] + y_ref[...]
    hg = _rms(x2, g_ref[...])
    gate = jax.nn.sigmoid(_dot(hg.astype(BF16), wg_ref[...]))
    o_ref[...] = x2 + gate * _dot(p_ref[...].astype(BF16), wp_ref[...])


def _ple(x1, y, p2d, g, wg, wp, tm):
    t, d = x1.shape
    row = lambda w: pl.BlockSpec((tm, w), lambda i: (i, 0))
    full = lambda a: pl.BlockSpec(a.shape, lambda i: (0,) * a.ndim)
    return pl.pallas_call(
        _ple_kernel,
        grid=(t // tm,),
        in_specs=[row(d), row(d), row(p2d.shape[1]), full(g), full(wg), full(wp)],
        out_specs=row(d),
        out_shape=jax.ShapeDtypeStruct((t, d), F32),
        compiler_params=pltpu.CompilerParams(
            dimension_semantics=("parallel",), vmem_limit_bytes=VMEM_LIMIT_DENSE),
        name="ple",
    )(x1, y, p2d, g, wg, wp)


def _expert_table(emb):
    n, d = emb.shape
    return emb.astype(BF16).reshape(n * (d // LANES), LANES)


def _layer(x, p, g_mix, w_in, q_norm, k_norm, sinks, w_out, g_ffn, w_query, sub_keys, emb_u, emb_v,
           g_ple, w_gate, w_proj):
    b, s, d = x.shape
    t = b * s
    tm = 256 if t % 256 == 0 else CHUNK
    tb = 64 if t % 64 == 0 else 8
    x2d = x.reshape(t, d)
    proj = _in_proj(x2d, g_mix[None, :], w_in.astype(BF16), tm)
    proj3 = proj.reshape(b, s, proj.shape[1])
    ro = _retention(proj3, _retention_tables(s))
    so = _swa(proj3, sinks, q_norm[None, :], k_norm[None, :])
    wo = w_out.astype(BF16)
    x1, h2, qp = _out_proj(x2d, ro.reshape(t, RET_WIDTH), so.reshape(t, SWA_Q_WIDTH),
                           wo[:RET_WIDTH], wo[RET_WIDTH:], g_ffn[None, :], w_query.astype(BF16), tm)
    idx, gate = _peer_topk(qp, sub_keys.astype(BF16), CHUNK)
    h3 = h2.reshape(t, ROWS_PER_EXPERT, LANES)
    act = _peer_u(idx, h3, gate, _expert_table(emb_u), tb)
    y = _peer_v(idx, act, _expert_table(emb_v), tb)
    out = _ple(x1, y.reshape(t, d), p.reshape(t, p.shape[-1]), g_ple[None, :],
               w_gate.astype(BF16), w_proj.astype(BF16), tm)
    return out.reshape(b, s, d)


def kernel(x, p, g_mix, w_in, q_norm, k_norm, sinks, w_out, g_ffn, peer_w_query, peer_sub_keys,
           peer_u, peer_v, g_ple, w_ple_gate, w_ple_proj):
    for i in range(p.shape[0]):
        x = _layer(x, p[i], g_mix[i], w_in[i], q_norm[i], k_norm[i], sinks[i], w_out[i], g_ffn[i],
                   peer_w_query[i], peer_sub_keys[i], peer_u[i], peer_v[i], g_ple[i], w_ple_gate[i],
                   w_ple_proj[i])
    return x
```

```python
import functools

import jax
import jax.numpy as jnp
from jax import lax
from jax.experimental import pallas as pl
from jax.experimental.pallas import tpu as pltpu

F32 = jnp.float32
BF16 = jnp.bfloat16

HEAD_DIM = 64
RET_HEADS = 8
SWA_Q_HEADS = 8
SWA_KV_HEADS = 2
SWA_GROUP = SWA_Q_HEADS // SWA_KV_HEADS
RET_WIDTH = RET_HEADS * HEAD_DIM
SWA_Q_WIDTH = SWA_Q_HEADS * HEAD_DIM
SWA_KV_WIDTH = SWA_KV_HEADS * HEAD_DIM
CHUNK = 128
ROPE_BASE = 10000.0
PEER_HEADS = 8
PEER_KEYS = 128
PEER_HALF = 128
PEER_TOPK = 16
PEER_PAIRS = PEER_HEADS * PEER_TOPK
EPS = 1e-6

LANES = 128
F32_SUBLANES = 8
BF16_SUBLANES = 16
VMEM_LIMIT_DENSE = 48 * 1024 * 1024
VMEM_LIMIT_TABLE = 56 * 1024 * 1024

ROWS_PER_EXPERT = 1024 // LANES
TILE_ROWS = BF16_SUBLANES


def _rms(x, g):
    ms = jnp.mean(x * x, axis=-1, keepdims=True)
    return x * lax.rsqrt(ms + EPS) * g


def _dot(a, b):
    return jnp.dot(a, b, preferred_element_type=F32)


def _dot_nt(a, b):
    return lax.dot_general(a, b, (((1,), (1,)), ((), ())), preferred_element_type=F32)


def _dot_tn(a, b):
    return lax.dot_general(a, b, (((0,), (0,)), ((), ())), preferred_element_type=F32)


def _in_proj_kernel(x_ref, g_ref, w_ref, o_ref):
    h = _rms(x_ref[...], g_ref[...])
    o_ref[...] = _dot(h.astype(BF16), w_ref[...])


def _in_proj(x2d, g, w, tm):
    t, d = x2d.shape
    n = w.shape[1]
    return pl.pallas_call(
        _in_proj_kernel,
        grid=(t // tm,),
        in_specs=[
            pl.BlockSpec((tm, d), lambda i: (i, 0)),
            pl.BlockSpec((1, d), lambda i: (0, 0)),
            pl.BlockSpec((d, n), lambda i: (0, 0)),
        ],
        out_specs=pl.BlockSpec((tm, n), lambda i: (i, 0)),
        out_shape=jax.ShapeDtypeStruct((t, n), F32),
        compiler_params=pltpu.CompilerParams(
            dimension_semantics=("parallel",), vmem_limit_bytes=VMEM_LIMIT_DENSE),
        name="in_proj",
    )(x2d, g, w)


def _retention_kernel(cd_ref, q_ref, k_ref, v_ref, g_ref, cos_ref, sin_ref, dec_ref, wk_ref, wq_ref,
                      o_ref, state_ref):
    n = pl.program_id(1)

    @pl.when(n == 0)
    def _():
        state_ref[...] = jnp.zeros_like(state_ref)

    c = cos_ref[...]
    s = sin_ref[...]
    lane = lax.broadcasted_iota(jnp.int32, (CHUNK, RET_WIDTH), 1)
    first_half = (lane % HEAD_DIM) < (HEAD_DIM // 2)

    def rot(x):
        partner = jnp.where(first_half,
                            pltpu.roll(x, RET_WIDTH - HEAD_DIM // 2, 1),
                            pltpu.roll(x, HEAD_DIM // 2, 1))
        return x * c + partner * s

    q = rot(q_ref[...])
    k = rot(k_ref[...]) * (HEAD_DIM ** -0.5)
    kw = k * wk_ref[...]
    v = v_ref[...]
    gate = g_ref[...]
    wq = wq_ref[...]
    for h in range(RET_HEADS):
        sl = slice(h * HEAD_DIM, (h + 1) * HEAD_DIM)
        qh = q[:, sl].astype(BF16)
        kh = k[:, sl].astype(BF16)
        vh = v[:, sl].astype(BF16)
        att = _dot_nt(qh, kh) * dec_ref[h]
        o = _dot(att.astype(BF16), vh)
        st = state_ref[h]
        o = o + _dot(qh, st.astype(BF16)) * wq[:, sl]
        state_ref[h] = st * cd_ref[h] + _dot_tn(kw[:, sl].astype(BF16), vh)
        mu = jnp.mean(o, axis=-1, keepdims=True)
        oc = o - mu
        var = jnp.mean(oc * oc, axis=-1, keepdims=True)
        on = oc * lax.rsqrt(var + EPS)
        gh = gate[:, sl]
        o_ref[:, sl] = gh * jax.nn.sigmoid(gh) * on


def _retention(proj3, tables):
    b, s, _ = proj3.shape
    cd, cos_t, sin_t, dec, wk, wq = tables
    col = lambda j: pl.BlockSpec((None, CHUNK, RET_WIDTH), lambda bi, ni, j=j: (bi, ni, j))
    pos_spec = pl.BlockSpec((CHUNK, RET_WIDTH), lambda bi, ni: (ni, 0))
    const2 = pl.BlockSpec((CHUNK, RET_WIDTH), lambda bi, ni: (0, 0))
    return pl.pallas_call(
        _retention_kernel,
        grid=(b, s // CHUNK),
        in_specs=[
            pl.BlockSpec(memory_space=pltpu.SMEM),
            col(0), col(1), col(2), col(3),
            pos_spec, pos_spec,
            pl.BlockSpec((RET_HEADS, CHUNK, CHUNK), lambda bi, ni: (0, 0, 0)),
            const2, const2,
        ],
        out_specs=pl.BlockSpec((None, CHUNK, RET_WIDTH), lambda bi, ni: (bi, ni, 0)),
        out_shape=jax.ShapeDtypeStruct((b, s, RET_WIDTH), F32),
        scratch_shapes=[pltpu.VMEM((RET_HEADS, HEAD_DIM, HEAD_DIM), F32)],
        compiler_params=pltpu.CompilerParams(
            dimension_semantics=("parallel", "arbitrary"), vmem_limit_bytes=VMEM_LIMIT_DENSE),
        name="retention",
    )(cd, proj3, proj3, proj3, proj3, cos_t, sin_t, dec, wk, wq)


def _retention_tables(s):
    half = HEAD_DIM // 2
    pos = jnp.arange(s, dtype=F32)
    freqs = ROPE_BASE ** (-jnp.arange(half, dtype=F32) / half)
    ang = pos[:, None] * freqs[None, :]
    c = jnp.cos(ang)
    sn = jnp.sin(ang)
    cos_t = jnp.tile(jnp.concatenate([c, c], axis=-1), (1, RET_HEADS))
    sin_t = jnp.tile(jnp.concatenate([-sn, sn], axis=-1), (1, RET_HEADS))
    gamma = 1.0 - jnp.exp2(-5.0 - jnp.arange(RET_HEADS, dtype=F32))
    log_g = jnp.log(gamma)
    idx = jnp.arange(CHUNK)
    rel = idx[:, None] - idx[None, :]
    dec = jnp.where(rel[None] >= 0,
                    jnp.exp(log_g[:, None, None] * jnp.maximum(rel, 0)[None].astype(F32)),
                    0.0)
    w_k = jnp.exp(log_g[None, :] * (CHUNK - 1 - idx)[:, None].astype(F32))
    w_q = jnp.exp(log_g[None, :] * (idx + 1)[:, None].astype(F32))
    cd = jnp.exp(log_g * CHUNK)
    wk = jnp.repeat(w_k, HEAD_DIM, axis=1)
    wq = jnp.repeat(w_q, HEAD_DIM, axis=1)
    return cd, cos_t, sin_t, dec, wk, wq


def _swa_kernel(sink_ref, q_ref, kc_ref, vc_ref, kp_ref, vp_ref, qn_ref, kn_ref, o_ref):
    n = pl.program_id(1)
    qi = lax.broadcasted_iota(jnp.int32, (CHUNK, 2 * CHUNK), 0)
    ki = lax.broadcasted_iota(jnp.int32, (CHUNK, 2 * CHUNK), 1)
    rel = qi + CHUNK - ki
    mask = (rel >= 0) & (rel < CHUNK) & ((n > 0) | (ki >= CHUNK))
    q = q_ref[...]
    qn = qn_ref[...]
    kn = kn_ref[...]
    for g in range(SWA_KV_HEADS):
        sl = slice(g * HEAD_DIM, (g + 1) * HEAD_DIM)
        kcat = jnp.concatenate([kp_ref[:, sl], kc_ref[:, sl]], axis=0)
        kcat = _rms(kcat, kn).astype(BF16)
        vcat = jnp.concatenate([vp_ref[:, sl], vc_ref[:, sl]], axis=0).astype(BF16)
        for j in range(SWA_GROUP):
            h = g * SWA_GROUP + j
            hs = slice(h * HEAD_DIM, (h + 1) * HEAD_DIM)
            qh = _rms(q[:, hs], qn).astype(BF16)
            sc = _dot_nt(qh, kcat) * (HEAD_DIM ** -0.5)
            sc = jnp.where(mask, sc, -jnp.inf)
            sink = sink_ref[h]
            m = jnp.maximum(jnp.max(sc, axis=-1, keepdims=True), sink)
            e = jnp.exp(sc - m)
            pr = e / (jnp.sum(e, axis=-1, keepdims=True) + jnp.exp(sink - m))
            o_ref[:, hs] = _dot(pr.astype(BF16), vcat)


def _swa(proj3, sinks, qn, kn):
    b, s, _ = proj3.shape
    q_blk = (4 * RET_WIDTH) // SWA_Q_WIDTH
    k_blk = (4 * RET_WIDTH + SWA_Q_WIDTH) // SWA_KV_WIDTH
    v_blk = k_blk + 1
    cur = lambda j: pl.BlockSpec((None, CHUNK, SWA_KV_WIDTH), lambda bi, ni, j=j: (bi, ni, j))
    prev = lambda j: pl.BlockSpec((None, CHUNK, SWA_KV_WIDTH),
                                  lambda bi, ni, j=j: (bi, jnp.maximum(ni - 1, 0), j))
    return pl.pallas_call(
        _swa_kernel,
        grid=(b, s // CHUNK),
        in_specs=[
            pl.BlockSpec(memory_space=pltpu.SMEM),
            pl.BlockSpec((None, CHUNK, SWA_Q_WIDTH), lambda bi, ni: (bi, ni, q_blk)),
            cur(k_blk), cur(v_blk), prev(k_blk), prev(v_blk),
            pl.BlockSpec((1, HEAD_DIM), lambda bi, ni: (0, 0)),
            pl.BlockSpec((1, HEAD_DIM), lambda bi, ni: (0, 0)),
        ],
        out_specs=pl.BlockSpec((None, CHUNK, SWA_Q_WIDTH), lambda bi, ni: (bi, ni, 0)),
        out_shape=jax.ShapeDtypeStruct((b, s, SWA_Q_WIDTH), F32),
        compiler_params=pltpu.CompilerParams(
            dimension_semantics=("parallel", "parallel"), vmem_limit_bytes=VMEM_LIMIT_DENSE),
        name="swa",
    )(sinks, proj3, proj3, proj3, proj3, proj3, qn, kn)


def _out_proj_kernel(x_ref, ro_ref, so_ref, wo1_ref, wo2_ref, g_ref, wq_ref, x1_ref, h2_ref, qp_ref):
    x1 = (x_ref[...] + _dot(ro_ref[...].astype(BF16), wo1_ref[...])
          + _dot(so_ref[...].astype(BF16), wo2_ref[...]))
    x1_ref[...] = x1
    h2 = _rms(x1, g_ref[...])
    h2_ref[...] = h2
    qp = _dot(h2.astype(BF16), wq_ref[...])
    width = qp_ref.shape[2]
    for h in range(PEER_HEADS):
        qp_ref[h] = qp[:, h * width:(h + 1) * width]


def _out_proj(x2d, ro, so, wo1, wo2, g, wq, tm):
    t, d = x2d.shape
    nq = wq.shape[1]
    width = nq // PEER_HEADS
    row = lambda w: pl.BlockSpec((tm, w), lambda i: (i, 0))
    full = lambda a: pl.BlockSpec(a.shape, lambda i: (0,) * a.ndim)
    return pl.pallas_call(
        _out_proj_kernel,
        grid=(t // tm,),
        in_specs=[row(d), row(ro.shape[1]), row(so.shape[1]), full(wo1), full(wo2), full(g), full(wq)],
        out_specs=[row(d), row(d), pl.BlockSpec((PEER_HEADS, tm, width), lambda i: (0, i, 0))],
        out_shape=[jax.ShapeDtypeStruct((t, d), F32), jax.ShapeDtypeStruct((t, d), F32),
                   jax.ShapeDtypeStruct((PEER_HEADS, t, width), F32)],
        compiler_params=pltpu.CompilerParams(
            dimension_semantics=("parallel",), vmem_limit_bytes=VMEM_LIMIT_DENSE),
        name="out_proj",
    )(x2d, ro, so, wo1, wo2, g, wq)


_CAND_B = (16, 8, 8, 8, 8, 8, 8, 8)


def _candidate_flat_index():
    pos = [a * PEER_TOPK + b for a, nb in enumerate(_CAND_B) for b in range(nb)]
    pos += [a * PEER_TOPK for a in range(len(_CAND_B), PEER_TOPK)]
    return pos


def _peer_topk_kernel(qp_ref, keys_ref, pos_ref, tile_ref, par_ref, gate_ref,
                      s_ref, i_ref, best_ref, exp_ref, et_ref, gt_ref):
    h = pl.program_id(1)
    tm = qp_ref.shape[0]
    neg = -jnp.inf
    key_id = lax.broadcasted_iota(jnp.int32, (PEER_KEYS, tm), 0).astype(F32)
    for p in range(2):
        qhp = qp_ref[:, p * PEER_HALF:(p + 1) * PEER_HALF].astype(BF16)
        sc = _dot_nt(keys_ref[p], qhp)
        for a in range(PEER_TOPK):
            m = jnp.max(sc, axis=0, keepdims=True)
            i = jnp.min(jnp.where(sc == m, key_id, float(PEER_KEYS)), axis=0, keepdims=True)
            sc = jnp.where(key_id == i, neg, sc)
            s_ref[p, a:a + 1, :] = m
            i_ref[p, a:a + 1, :] = i
    s0, s1 = s_ref[0], s_ref[1]
    i0, i1 = i_ref[0] * float(PEER_KEYS), i_ref[1]
    cs_blocks, ci_blocks = [], []
    for a, nb in enumerate(_CAND_B):
        cs_blocks.append(s0[a:a + 1] + s1[0:nb])
        ci_blocks.append(i0[a:a + 1] + i1[0:nb])
    na = len(_CAND_B)
    cs_blocks.append(s0[na:] + s1[0:1])
    ci_blocks.append(i0[na:] + i1[0:1])
    cs = jnp.concatenate(cs_blocks, axis=0)
    ci = jnp.concatenate(ci_blocks, axis=0)
    pos = pos_ref[...]
    for k in range(PEER_TOPK):
        m = jnp.max(cs, axis=0, keepdims=True)
        first = jnp.min(jnp.where(cs == m, pos, float(PEER_TOPK * PEER_TOPK)), axis=0, keepdims=True)
        sel = pos == first
        best_ref[k:k + 1, :] = m
        exp_ref[k:k + 1, :] = jnp.max(jnp.where(sel, ci, -1.0), axis=0, keepdims=True)
        cs = jnp.where(sel, neg, cs)
    best = best_ref[...]
    ex = jnp.exp(best - best[0:1])
    gate = ex / jnp.sum(ex, axis=0, keepdims=True)
    rows = pl.ds(pl.multiple_of(h * PEER_TOPK, PEER_TOPK), PEER_TOPK)
    et_ref[rows, :] = exp_ref[...]
    gt_ref[rows, :] = gate

    @pl.when(h == pl.num_programs(1) - 1)
    def _():
        expert = et_ref[...].T.astype(jnp.int32)
        tile_ref[...] = expert >> 1
        par_ref[...] = (expert & 1).astype(F32)
        gate_ref[...] = gt_ref[...].T


def _peer_topk(qp, keys, tm):
    nh, t, width = qp.shape
    pos = _candidate_flat_index()
    pos = jnp.broadcast_to(jnp.asarray(pos, F32)[:, None], (len(pos), tm))
    full = lambda a: pl.BlockSpec(a.shape, lambda i, h: (0,) * a.ndim)
    out_spec = pl.BlockSpec((tm, PEER_PAIRS), lambda i, h: (i, 0))
    return pl.pallas_call(
        _peer_topk_kernel,
        grid=(t // tm, nh),
        in_specs=[pl.BlockSpec((None, tm, width), lambda i, h: (h, i, 0)), full(keys), full(pos)],
        out_specs=[out_spec, out_spec, out_spec],
        out_shape=[jax.ShapeDtypeStruct((t, PEER_PAIRS), jnp.int32),
                   jax.ShapeDtypeStruct((t, PEER_PAIRS), F32),
                   jax.ShapeDtypeStruct((t, PEER_PAIRS), F32)],
        scratch_shapes=[pltpu.VMEM((2, PEER_TOPK, tm), F32), pltpu.VMEM((2, PEER_TOPK, tm), F32),
                        pltpu.VMEM((PEER_TOPK, tm), F32), pltpu.VMEM((PEER_TOPK, tm), F32),
                        pltpu.VMEM((PEER_PAIRS, tm), F32), pltpu.VMEM((PEER_PAIRS, tm), F32)],
        compiler_params=pltpu.CompilerParams(
            dimension_semantics=("parallel", "arbitrary"), vmem_limit_bytes=VMEM_LIMIT_DENSE),
        name="peer_topk",
    )(qp, keys, pos)


SLOTS = PEER_PAIRS // 2
SLOT_ROWS = SLOTS * TILE_ROWS
MXU_ROWS = 256
SLOTS_PER_MXU_TILE = MXU_ROWS // TILE_ROWS
TOKENS_PER_STEP = 2


def _gather_slots(tile_ref, tab_ref, t, scale=None):
    blocks = []
    for i in range(SLOTS):
        a = tab_ref[tile_ref[t, 2 * i]]
        b = tab_ref[tile_ref[t, 2 * i + 1]]
        if scale is not None:
            a = a * scale
            b = b * scale
        blocks.append(jnp.concatenate([a, b], axis=1))
    return jnp.concatenate(blocks, axis=0)


def _peer_u_kernel(tile_ref, h_ref, par_ref, gate_ref, half_ref, tab_ref, act_ref, za_ref, zb_ref):
    tb = h_ref.shape[0]
    ones = jnp.ones((BF16_SUBLANES, LANES), BF16)
    half_sum = half_ref[...]

    def token(t):
        ht = h_ref[t].astype(BF16)
        prod = _gather_slots(tile_ref, tab_ref, t, scale=jnp.concatenate([ht, ht], axis=0))
        z = [_dot(half_sum, prod[k * MXU_ROWS:(k + 1) * MXU_ROWS]) for k in range(SLOT_ROWS // MXU_ROWS)]
        z = jnp.concatenate(z, axis=0).astype(BF16)
        za_ref[pl.ds(t, 1), :] = _dot_nt(ones, z[:, :LANES])[0:1]
        zb_ref[pl.ds(t, 1), :] = _dot_nt(ones, z[:, LANES:])[0:1]

    def step(s, carry):
        for u in range(TOKENS_PER_STEP):
            token(s * TOKENS_PER_STEP + u)
        return carry

    lax.fori_loop(0, tb // TOKENS_PER_STEP, step, 0)
    za = za_ref[...]
    zb = zb_ref[...]
    first = par_ref[...] == 0.0
    even_pair = jnp.where(first, za, pltpu.roll(za, LANES - 1, 1))
    odd_pair = jnp.where(first, pltpu.roll(zb, 1, 1), zb)
    lane = lax.broadcasted_iota(jnp.int32, za.shape, 1)
    a = jnp.where(lane % 2 == 0, even_pair, odd_pair)
    act_ref[...] = 0.5 * a * (1.0 + lax.erf(a * (2.0 ** -0.5))) * gate_ref[...]


def _peer_u(tile, h3, par, gate, tab, tb):
    t = tile.shape[0]
    r = jnp.arange(2 * SLOTS_PER_MXU_TILE)[:, None]
    c = jnp.arange(MXU_ROWS)[None, :]
    half_sum = ((c // TILE_ROWS == r // 2) & ((c % TILE_ROWS) // ROWS_PER_EXPERT == r % 2)).astype(BF16)
    row = pl.BlockSpec((tb, PEER_PAIRS), lambda i: (i, 0))
    return pl.pallas_call(
        _peer_u_kernel,
        grid=(t // tb,),
        in_specs=[
            pl.BlockSpec((tb, PEER_PAIRS), lambda i: (i, 0), memory_space=pltpu.SMEM),
            pl.BlockSpec((tb, ROWS_PER_EXPERT, LANES), lambda i: (i, 0, 0)),
            row, row,
            pl.BlockSpec(half_sum.shape, lambda i: (0, 0)),
            pl.BlockSpec(memory_space=pltpu.VMEM),
        ],
        out_specs=row,
        out_shape=jax.ShapeDtypeStruct((t, PEER_PAIRS), F32),
        scratch_shapes=[pltpu.VMEM((tb, PEER_PAIRS), F32), pltpu.VMEM((tb, PEER_PAIRS), F32)],
        compiler_params=pltpu.CompilerParams(
            dimension_semantics=("parallel",), vmem_limit_bytes=VMEM_LIMIT_TABLE),
        name="peer_u",
    )(tile, h3, par, gate, half_sum, tab)


def _peer_v_kernel(tile_ref, act_ref, par_ref, exp_ref, tab_ref, y_ref, coef_ref):
    tb = act_ref.shape[0]
    act = act_ref[...]
    hi = act.astype(BF16)
    lo = (act - hi.astype(F32)).astype(BF16)
    lhs = jnp.concatenate([hi, lo, par_ref[...].astype(BF16)], axis=0)
    coef_ref[...] = _dot(lhs, exp_ref[...])
    shape = (ROWS_PER_EXPERT, SLOT_ROWS)
    q_minus_r = (lax.broadcasted_iota(jnp.int32, shape, 1) % TILE_ROWS
                 - lax.broadcasted_iota(jnp.int32, shape, 0)).astype(F32)

    def token(t):
        def spread(kind, side):
            row = coef_ref[pl.ds(kind * tb + t, 1), side * SLOT_ROWS:(side + 1) * SLOT_ROWS]
            return jnp.broadcast_to(row, shape)

        parts = {}
        for side in range(2):
            mine = q_minus_r == spread(2, side) * float(ROWS_PER_EXPERT)
            for kind in range(2):
                parts[kind, side] = jnp.where(mine, spread(kind, side), 0.0)
        coef = jnp.concatenate([parts[0, 0], parts[0, 1], parts[1, 0], parts[1, 1]], axis=0).astype(BF16)
        res = _dot(coef, _gather_slots(tile_ref, tab_ref, t))
        n = ROWS_PER_EXPERT
        y_ref[t] = ((res[0:n, :LANES] + res[2 * n:3 * n, :LANES])
                    + (res[n:2 * n, LANES:] + res[3 * n:4 * n, LANES:]))

    def step(s, carry):
        for u in range(TOKENS_PER_STEP):
            token(s * TOKENS_PER_STEP + u)
        return carry

    lax.fori_loop(0, tb // TOKENS_PER_STEP, step, 0)


def _peer_v(tile, act, par, tab, tb):
    t = tile.shape[0]
    j = jnp.arange(PEER_PAIRS)[:, None]
    c = jnp.arange(2 * SLOT_ROWS)[None, :]
    expand = (2 * ((c % SLOT_ROWS) // TILE_ROWS) + c // SLOT_ROWS == j).astype(BF16)
    row = pl.BlockSpec((tb, PEER_PAIRS), lambda i: (i, 0))
    return pl.pallas_call(
        _peer_v_kernel,
        grid=(t // tb,),
        in_specs=[
            pl.BlockSpec((tb, PEER_PAIRS), lambda i: (i, 0), memory_space=pltpu.SMEM),
            row, row,
            pl.BlockSpec(expand.shape, lambda i: (0, 0)),
            pl.BlockSpec(memory_space=pltpu.VMEM),
        ],
        out_specs=pl.BlockSpec((tb, ROWS_PER_EXPERT, LANES), lambda i: (i, 0, 0)),
        out_shape=jax.ShapeDtypeStruct((t, ROWS_PER_EXPERT, LANES), F32),
        scratch_shapes=[pltpu.VMEM((3 * tb, 2 * SLOT_ROWS), F32)],
        compiler_params=pltpu.CompilerParams(
            dimension_semantics=("parallel",), vmem_limit_bytes=VMEM_LIMIT_TABLE),
        name="peer_v",
    )(tile, act, par, expand, tab)


def _ple_kernel(x1_ref, y_ref, p_ref, g_ref, wg_ref, wp_ref, o_ref):
    x2 = x1_ref[...] + y_ref[...]
    hg = _rms(x2, g_ref[...])
    gate = jax.nn.sigmoid(_dot(hg.astype(BF16), wg_ref[...]))
    o_ref[...] = x2 + gate * _dot(p_ref[...].astype(BF16), wp_ref[...])


def _ple(x1, y, p2d, g, wg, wp, tm):
    t, d = x1.shape
    row = lambda w: pl.BlockSpec((tm, w), lambda i: (i, 0))
    full = lambda a: pl.BlockSpec(a.shape, lambda i: (0,) * a.ndim)
    return pl.pallas_call(
        _ple_kernel,
        grid=(t // tm,),
        in_specs=[row(d), row(d), row(p2d.shape[1]), full(g), full(wg), full(wp)],
        out_specs=row(d),
        out_shape=jax.ShapeDtypeStruct((t, d), F32),
        compiler_params=pltpu.CompilerParams(
            dimension_semantics=("parallel",), vmem_limit_bytes=VMEM_LIMIT_DENSE),
        name="ple",
    )(x1, y, p2d, g, wg, wp)


def _expert_table(emb):
    n, d = emb.shape
    assert d == ROWS_PER_EXPERT * LANES and n % 2 == 0
    return emb.astype(BF16).reshape(n // 2, TILE_ROWS, LANES)


def _layer(x, p, g_mix, w_in, q_norm, k_norm, sinks, w_out, g_ffn, w_query, sub_keys, emb_u, emb_v,
           g_ple, w_gate, w_proj):
    b, s, d = x.shape
    t = b * s
    tm = 256 if t % 256 == 0 else CHUNK
    tb = 64 if t % 64 == 0 else 8
    x2d = x.reshape(t, d)
    proj = _in_proj(x2d, g_mix[None, :], w_in.astype(BF16), tm)
    proj3 = proj.reshape(b, s, proj.shape[1])
    ro = _retention(proj3, _retention_tables(s))
    so = _swa(proj3, sinks, q_norm[None, :], k_norm[None, :])
    wo = w_out.astype(BF16)
    x1, h2, qp = _out_proj(x2d, ro.reshape(t, RET_WIDTH), so.reshape(t, SWA_Q_WIDTH),
                           wo[:RET_WIDTH], wo[RET_WIDTH:], g_ffn[None, :], w_query.astype(BF16), tm)
    tile, par, gate = _peer_topk(qp, sub_keys.astype(BF16), tm)
    h3 = h2.reshape(t, ROWS_PER_EXPERT, LANES)
    act = _peer_u(tile, h3, par, gate, _expert_table(emb_u), tb)
    y = _peer_v(tile, act, par, _expert_table(emb_v), tb)
    out = _ple(x1, y.reshape(t, d), p.reshape(t, p.shape[-1]), g_ple[None, :],
               w_gate.astype(BF16), w_proj.astype(BF16), tm)
    return out.reshape(b, s, d)


def kernel(x, p, g_mix, w_in, q_norm, k_norm, sinks, w_out, g_ffn, peer_w_query, peer_sub_keys,
           peer_u, peer_v, g_ple, w_ple_gate, w_ple_proj):
    for i in range(p.shape[0]):
        x = _layer(x, p[i], g_mix[i], w_in[i], q_norm[i], k_norm[i], sinks[i], w_out[i], g_ffn[i],
                   peer_w_query[i], peer_sub_keys[i], peer_u[i], peer_v[i], g_ple[i], w_ple_gate[i],
                   w_ple_proj[i])
    return x
```

```python
import functools

import jax
import jax.numpy as jnp
from jax import lax
from jax.experimental import pallas as pl
from jax.experimental.pallas import tpu as pltpu

F32 = jnp.float32
BF16 = jnp.bfloat16

HEAD_DIM = 64
RET_HEADS = 8
SWA_Q_HEADS = 8
SWA_KV_HEADS = 2
SWA_GROUP = SWA_Q_HEADS // SWA_KV_HEADS
RET_WIDTH = RET_HEADS * HEAD_DIM
SWA_Q_WIDTH = SWA_Q_HEADS * HEAD_DIM
SWA_KV_WIDTH = SWA_KV_HEADS * HEAD_DIM
CHUNK = 128
ROPE_BASE = 10000.0
PEER_HEADS = 8
PEER_KEYS = 128
PEER_HALF = 128
PEER_TOPK = 16
PEER_PAIRS = PEER_HEADS * PEER_TOPK
EPS = 1e-6

LANES = 128
F32_SUBLANES = 8
BF16_SUBLANES = 16
VMEM_LIMIT_DENSE = 48 * 1024 * 1024
VMEM_LIMIT_TABLE = 56 * 1024 * 1024

ROWS_PER_EXPERT = 1024 // LANES
TILE_ROWS = BF16_SUBLANES


def _rms(x, g):
    ms = jnp.mean(x * x, axis=-1, keepdims=True)
    return x * lax.rsqrt(ms + EPS) * g


def _dot(a, b):
    return jnp.dot(a, b, preferred_element_type=F32)


def _dot_nt(a, b):
    return lax.dot_general(a, b, (((1,), (1,)), ((), ())), preferred_element_type=F32)


def _dot_tn(a, b):
    return lax.dot_general(a, b, (((0,), (0,)), ((), ())), preferred_element_type=F32)


def _in_proj_kernel(x_ref, g_ref, w_ref, o_ref):
    h = _rms(x_ref[...], g_ref[...])
    o_ref[...] = _dot(h.astype(BF16), w_ref[...])


def _in_proj(x2d, g, w, tm):
    t, d = x2d.shape
    n = w.shape[1]
    return pl.pallas_call(
        _in_proj_kernel,
        grid=(t // tm,),
        in_specs=[
            pl.BlockSpec((tm, d), lambda i: (i, 0)),
            pl.BlockSpec((1, d), lambda i: (0, 0)),
            pl.BlockSpec((d, n), lambda i: (0, 0)),
        ],
        out_specs=pl.BlockSpec((tm, n), lambda i: (i, 0)),
        out_shape=jax.ShapeDtypeStruct((t, n), F32),
        compiler_params=pltpu.CompilerParams(
            dimension_semantics=("parallel",), vmem_limit_bytes=VMEM_LIMIT_DENSE),
        name="in_proj",
    )(x2d, g, w)


def _retention_kernel(cd_ref, q_ref, k_ref, v_ref, g_ref, cos_ref, sin_ref, dec_ref, wk_ref, wq_ref,
                      o_ref, state_ref):
    n = pl.program_id(1)

    @pl.when(n == 0)
    def _():
        state_ref[...] = jnp.zeros_like(state_ref)

    c = cos_ref[...]
    s = sin_ref[...]
    lane = lax.broadcasted_iota(jnp.int32, (CHUNK, RET_WIDTH), 1)
    first_half = (lane % HEAD_DIM) < (HEAD_DIM // 2)

    def rot(x):
        partner = jnp.where(first_half,
                            pltpu.roll(x, RET_WIDTH - HEAD_DIM // 2, 1),
                            pltpu.roll(x, HEAD_DIM // 2, 1))
        return x * c + partner * s

    q = rot(q_ref[...])
    k = rot(k_ref[...]) * (HEAD_DIM ** -0.5)
    kw = k * wk_ref[...]
    v = v_ref[...]
    gate = g_ref[...]
    wq = wq_ref[...]
    for h in range(RET_HEADS):
        sl = slice(h * HEAD_DIM, (h + 1) * HEAD_DIM)
        qh = q[:, sl].astype(BF16)
        kh = k[:, sl].astype(BF16)
        vh = v[:, sl].astype(BF16)
        att = _dot_nt(qh, kh) * dec_ref[h]
        o = _dot(att.astype(BF16), vh)
        st = state_ref[h]
        o = o + _dot(qh, st.astype(BF16)) * wq[:, sl]
        state_ref[h] = st * cd_ref[h] + _dot_tn(kw[:, sl].astype(BF16), vh)
        mu = jnp.mean(o, axis=-1, keepdims=True)
        oc = o - mu
        var = jnp.mean(oc * oc, axis=-1, keepdims=True)
        on = oc * lax.rsqrt(var + EPS)
        gh = gate[:, sl]
        o_ref[:, sl] = gh * jax.nn.sigmoid(gh) * on


def _retention(proj3, tables):
    b, s, _ = proj3.shape
    cd, cos_t, sin_t, dec, wk, wq = tables
    col = lambda j: pl.BlockSpec((None, CHUNK, RET_WIDTH), lambda bi, ni, j=j: (bi, ni, j))
    pos_spec = pl.BlockSpec((CHUNK, RET_WIDTH), lambda bi, ni: (ni, 0))
    const2 = pl.BlockSpec((CHUNK, RET_WIDTH), lambda bi, ni: (0, 0))
    return pl.pallas_call(
        _retention_kernel,
        grid=(b, s // CHUNK),
        in_specs=[
            pl.BlockSpec(memory_space=pltpu.SMEM),
            col(0), col(1), col(2), col(3),
            pos_spec, pos_spec,
            pl.BlockSpec((RET_HEADS, CHUNK, CHUNK), lambda bi, ni: (0, 0, 0)),
            const2, const2,
        ],
        out_specs=pl.BlockSpec((None, CHUNK, RET_WIDTH), lambda bi, ni: (bi, ni, 0)),
        out_shape=jax.ShapeDtypeStruct((b, s, RET_WIDTH), F32),
        scratch_shapes=[pltpu.VMEM((RET_HEADS, HEAD_DIM, HEAD_DIM), F32)],
        compiler_params=pltpu.CompilerParams(
            dimension_semantics=("parallel", "arbitrary"), vmem_limit_bytes=VMEM_LIMIT_DENSE),
        name="retention",
    )(cd, proj3, proj3, proj3, proj3, cos_t, sin_t, dec, wk, wq)


def _retention_tables(s):
    half = HEAD_DIM // 2
    pos = jnp.arange(s, dtype=F32)
    freqs = ROPE_BASE ** (-jnp.arange(half, dtype=F32) / half)
    ang = pos[:, None] * freqs[None, :]
    c = jnp.cos(ang)
    sn = jnp.sin(ang)
    cos_t = jnp.tile(jnp.concatenate([c, c], axis=-1), (1, RET_HEADS))
    sin_t = jnp.tile(jnp.concatenate([-sn, sn], axis=-1), (1, RET_HEADS))
    gamma = 1.0 - jnp.exp2(-5.0 - jnp.arange(RET_HEADS, dtype=F32))
    log_g = jnp.log(gamma)
    idx = jnp.arange(CHUNK)
    rel = idx[:, None] - idx[None, :]
    dec = jnp.where(rel[None] >= 0,
                    jnp.exp(log_g[:, None, None] * jnp.maximum(rel, 0)[None].astype(F32)),
                    0.0)
    w_k = jnp.exp(log_g[None, :] * (CHUNK - 1 - idx)[:, None].astype(F32))
    w_q = jnp.exp(log_g[None, :] * (idx + 1)[:, None].astype(F32))
    cd = jnp.exp(log_g * CHUNK)
    wk = jnp.repeat(w_k, HEAD_DIM, axis=1)
    wq = jnp.repeat(w_q, HEAD_DIM, axis=1)
    return cd, cos_t, sin_t, dec, wk, wq


def _swa_kernel(sink_ref, q_ref, kc_ref, vc_ref, kp_ref, vp_ref, qn_ref, kn_ref, o_ref):
    n = pl.program_id(1)
    qi = lax.broadcasted_iota(jnp.int32, (CHUNK, 2 * CHUNK), 0)
    ki = lax.broadcasted_iota(jnp.int32, (CHUNK, 2 * CHUNK), 1)
    rel = qi + CHUNK - ki
    mask = (rel >= 0) & (rel < CHUNK) & ((n > 0) | (ki >= CHUNK))
    q = q_ref[...]
    qn = qn_ref[...]
    kn = kn_ref[...]
    for g in range(SWA_KV_HEADS):
        sl = slice(g * HEAD_DIM, (g + 1) * HEAD_DIM)
        kcat = jnp.concatenate([kp_ref[:, sl], kc_ref[:, sl]], axis=0)
        kcat = _rms(kcat, kn).astype(BF16)
        vcat = jnp.concatenate([vp_ref[:, sl], vc_ref[:, sl]], axis=0).astype(BF16)
        for j in range(SWA_GROUP):
            h = g * SWA_GROUP + j
            hs = slice(h * HEAD_DIM, (h + 1) * HEAD_DIM)
            qh = _rms(q[:, hs], qn).astype(BF16)
            sc = _dot_nt(qh, kcat) * (HEAD_DIM ** -0.5)
            sc = jnp.where(mask, sc, -jnp.inf)
            sink = sink_ref[h]
            m = jnp.maximum(jnp.max(sc, axis=-1, keepdims=True), sink)
            e = jnp.exp(sc - m)
            pr = e / (jnp.sum(e, axis=-1, keepdims=True) + jnp.exp(sink - m))
            o_ref[:, hs] = _dot(pr.astype(BF16), vcat)


def _swa(proj3, sinks, qn, kn):
    b, s, _ = proj3.shape
    q_blk = (4 * RET_WIDTH) // SWA_Q_WIDTH
    k_blk = (4 * RET_WIDTH + SWA_Q_WIDTH) // SWA_KV_WIDTH
    v_blk = k_blk + 1
    cur = lambda j: pl.BlockSpec((None, CHUNK, SWA_KV_WIDTH), lambda bi, ni, j=j: (bi, ni, j))
    prev = lambda j: pl.BlockSpec((None, CHUNK, SWA_KV_WIDTH),
                                  lambda bi, ni, j=j: (bi, jnp.maximum(ni - 1, 0), j))
    return pl.pallas_call(
        _swa_kernel,
        grid=(b, s // CHUNK),
        in_specs=[
            pl.BlockSpec(memory_space=pltpu.SMEM),
            pl.BlockSpec((None, CHUNK, SWA_Q_WIDTH), lambda bi, ni: (bi, ni, q_blk)),
            cur(k_blk), cur(v_blk), prev(k_blk), prev(v_blk),
            pl.BlockSpec((1, HEAD_DIM), lambda bi, ni: (0, 0)),
            pl.BlockSpec((1, HEAD_DIM), lambda bi, ni: (0, 0)),
        ],
        out_specs=pl.BlockSpec((None, CHUNK, SWA_Q_WIDTH), lambda bi, ni: (bi, ni, 0)),
        out_shape=jax.ShapeDtypeStruct((b, s, SWA_Q_WIDTH), F32),
        compiler_params=pltpu.CompilerParams(
            dimension_semantics=("parallel", "parallel"), vmem_limit_bytes=VMEM_LIMIT_DENSE),
        name="swa",
    )(sinks, proj3, proj3, proj3, proj3, proj3, qn, kn)


def _out_proj_kernel(x_ref, ro_ref, so_ref, wo1_ref, wo2_ref, g_ref, wq_ref, x1_ref, h2_ref, qp_ref):
    x1 = (x_ref[...] + _dot(ro_ref[...].astype(BF16), wo1_ref[...])
          + _dot(so_ref[...].astype(BF16), wo2_ref[...]))
    x1_ref[...] = x1
    h2 = _rms(x1, g_ref[...])
    h2_ref[...] = h2
    qp = _dot(h2.astype(BF16), wq_ref[...])
    width = qp_ref.shape[2]
    for h in range(PEER_HEADS):
        qp_ref[h] = qp[:, h * width:(h + 1) * width]


def _out_proj(x2d, ro, so, wo1, wo2, g, wq, tm):
    t, d = x2d.shape
    nq = wq.shape[1]
    width = nq // PEER_HEADS
    row = lambda w: pl.BlockSpec((tm, w), lambda i: (i, 0))
    full = lambda a: pl.BlockSpec(a.shape, lambda i: (0,) * a.ndim)
    return pl.pallas_call(
        _out_proj_kernel,
        grid=(t // tm,),
        in_specs=[row(d), row(ro.shape[1]), row(so.shape[1]), full(wo1), full(wo2), full(g), full(wq)],
        out_specs=[row(d), row(d), pl.BlockSpec((PEER_HEADS, tm, width), lambda i: (0, i, 0))],
        out_shape=[jax.ShapeDtypeStruct((t, d), F32), jax.ShapeDtypeStruct((t, d), F32),
                   jax.ShapeDtypeStruct((PEER_HEADS, t, width), F32)],
        compiler_params=pltpu.CompilerParams(
            dimension_semantics=("parallel",), vmem_limit_bytes=VMEM_LIMIT_DENSE),
        name="out_proj",
    )(x2d, ro, so, wo1, wo2, g, wq)


_CAND_B = (16, 8, 8, 8, 8, 8, 8, 8)


def _candidate_flat_index():
    pos = [a * PEER_TOPK + b for a, nb in enumerate(_CAND_B) for b in range(nb)]
    pos += [a * PEER_TOPK for a in range(len(_CAND_B), PEER_TOPK)]
    return pos


def _peer_topk_kernel(qp_ref, keys_ref, pos_ref, tile_ref, par_ref, gate_ref,
                      s_ref, i_ref, best_ref, exp_ref, et_ref, gt_ref):
    h = pl.program_id(1)
    tm = qp_ref.shape[0]
    neg = -jnp.inf
    key_id = lax.broadcasted_iota(jnp.int32, (PEER_KEYS, tm), 0).astype(F32)
    for p in range(2):
        qhp = qp_ref[:, p * PEER_HALF:(p + 1) * PEER_HALF].astype(BF16)
        sc = _dot_nt(keys_ref[p], qhp)
        for a in range(PEER_TOPK):
            m = jnp.max(sc, axis=0, keepdims=True)
            i = jnp.min(jnp.where(sc == m, key_id, float(PEER_KEYS)), axis=0, keepdims=True)
            sc = jnp.where(key_id == i, neg, sc)
            s_ref[p, a:a + 1, :] = m
            i_ref[p, a:a + 1, :] = i
    s0, s1 = s_ref[0], s_ref[1]
    i0, i1 = i_ref[0] * float(PEER_KEYS), i_ref[1]
    cs_blocks, ci_blocks = [], []
    for a, nb in enumerate(_CAND_B):
        cs_blocks.append(s0[a:a + 1] + s1[0:nb])
        ci_blocks.append(i0[a:a + 1] + i1[0:nb])
    na = len(_CAND_B)
    cs_blocks.append(s0[na:] + s1[0:1])
    ci_blocks.append(i0[na:] + i1[0:1])
    cs = jnp.concatenate(cs_blocks, axis=0)
    ci = jnp.concatenate(ci_blocks, axis=0)
    pos = pos_ref[...]
    for k in range(PEER_TOPK):
        m = jnp.max(cs, axis=0, keepdims=True)
        first = jnp.min(jnp.where(cs == m, pos, float(PEER_TOPK * PEER_TOPK)), axis=0, keepdims=True)
        sel = pos == first
        best_ref[k:k + 1, :] = m
        exp_ref[k:k + 1, :] = jnp.max(jnp.where(sel, ci, -1.0), axis=0, keepdims=True)
        cs = jnp.where(sel, neg, cs)
    best = best_ref[...]
    ex = jnp.exp(best - best[0:1])
    gate = ex / jnp.sum(ex, axis=0, keepdims=True)
    rows = pl.ds(pl.multiple_of(h * PEER_TOPK, PEER_TOPK), PEER_TOPK)
    et_ref[rows, :] = exp_ref[...]
    gt_ref[rows, :] = gate

    @pl.when(h == pl.num_programs(1) - 1)
    def _():
        expert = et_ref[...].T.astype(jnp.int32)
        tile_ref[...] = (expert >> 1) * WORD_ROWS
        par_ref[...] = (expert & 1).astype(F32)
        gate_ref[...] = gt_ref[...].T


def _peer_topk(qp, keys, tm):
    nh, t, width = qp.shape
    pos = _candidate_flat_index()
    pos = jnp.broadcast_to(jnp.asarray(pos, F32)[:, None], (len(pos), tm))
    full = lambda a: pl.BlockSpec(a.shape, lambda i, h: (0,) * a.ndim)
    out_spec = pl.BlockSpec((tm, PEER_PAIRS), lambda i, h: (i, 0))
    return pl.pallas_call(
        _peer_topk_kernel,
        grid=(t // tm, nh),
        in_specs=[pl.BlockSpec((None, tm, width), lambda i, h: (h, i, 0)), full(keys), full(pos)],
        out_specs=[out_spec, out_spec, out_spec],
        out_shape=[jax.ShapeDtypeStruct((t, PEER_PAIRS), jnp.int32),
                   jax.ShapeDtypeStruct((t, PEER_PAIRS), F32),
                   jax.ShapeDtypeStruct((t, PEER_PAIRS), F32)],
        scratch_shapes=[pltpu.VMEM((2, PEER_TOPK, tm), F32), pltpu.VMEM((2, PEER_TOPK, tm), F32),
                        pltpu.VMEM((PEER_TOPK, tm), F32), pltpu.VMEM((PEER_TOPK, tm), F32),
                        pltpu.VMEM((PEER_PAIRS, tm), F32), pltpu.VMEM((PEER_PAIRS, tm), F32)],
        compiler_params=pltpu.CompilerParams(
            dimension_semantics=("parallel", "arbitrary"), vmem_limit_bytes=VMEM_LIMIT_DENSE),
        name="peer_topk",
    )(qp, keys, pos)


WORD_ROWS = TILE_ROWS // 2
SLOTS = PEER_PAIRS // 2
SLOT_ROWS = SLOTS * TILE_ROWS
MXU_ROWS = 256
SLOTS_PER_MXU_TILE = MXU_ROWS // TILE_ROWS
PEER_TOKEN_BLOCK = 16


def _gather_slots(tile_ref, tab_ref, t, scale=None):
    def tile(j):
        start = pl.multiple_of(tile_ref[t, j], WORD_ROWS)
        x = pltpu.bitcast(tab_ref[pl.ds(start, WORD_ROWS), :], BF16)
        return x if scale is None else x * scale

    blocks = []
    for k in range(SLOTS // SLOTS_PER_MXU_TILE):
        slots = [jnp.concatenate([tile(2 * i), tile(2 * i + 1)], axis=1)
                 for i in range(k * SLOTS_PER_MXU_TILE, (k + 1) * SLOTS_PER_MXU_TILE)]
        blocks.append(jnp.concatenate(slots, axis=0))
    return blocks


def _peer_u_kernel(tile_ref, h_ref, par_ref, gate_ref, half_ref, tab_ref, act_ref, za_ref, zb_ref):
    tb = h_ref.shape[0]
    ones = jnp.ones((BF16_SUBLANES, LANES), BF16)
    half_sum = half_ref[...]

    def gather(t):
        ht = h_ref[t].astype(BF16)
        return _gather_slots(tile_ref, tab_ref, t, scale=jnp.concatenate([ht, ht], axis=0))

    def half_sums(prod):
        z = [_dot(half_sum, block) for block in prod]
        return jnp.concatenate(z, axis=0).astype(BF16)

    def lane_sums(t, z):
        s = _dot_nt(ones, jnp.concatenate([z[:, :LANES], z[:, LANES:]], axis=0))
        za_ref[t:t + 1, :] = s[0:1, :PEER_PAIRS]
        zb_ref[t:t + 1, :] = s[0:1, PEER_PAIRS:]

    prod = gather(0)
    pending = None
    for t in range(tb):
        nxt = gather(t + 1) if t + 1 < tb else None
        z = half_sums(prod)
        if pending is not None:
            lane_sums(t - 1, pending)
        pending, prod = z, nxt
    lane_sums(tb - 1, pending)
    za = za_ref[...]
    zb = zb_ref[...]
    first = par_ref[...] == 0.0
    even_pair = jnp.where(first, za, pltpu.roll(za, LANES - 1, 1))
    odd_pair = jnp.where(first, pltpu.roll(zb, 1, 1), zb)
    lane = lax.broadcasted_iota(jnp.int32, za.shape, 1)
    a = jnp.where(lane % 2 == 0, even_pair, odd_pair)
    act_ref[...] = 0.5 * a * (1.0 + lax.erf(a * (2.0 ** -0.5))) * gate_ref[...]


def _peer_u(tile, h3, par, gate, tab, tb):
    t = tile.shape[0]
    r = jnp.arange(2 * SLOTS_PER_MXU_TILE)[:, None]
    c = jnp.arange(MXU_ROWS)[None, :]
    half_sum = ((c // TILE_ROWS == r // 2) & ((c % TILE_ROWS) // ROWS_PER_EXPERT == r % 2)).astype(BF16)
    row = pl.BlockSpec((tb, PEER_PAIRS), lambda i: (i, 0))
    return pl.pallas_call(
        _peer_u_kernel,
        grid=(t // tb,),
        in_specs=[
            pl.BlockSpec((tb, PEER_PAIRS), lambda i: (i, 0), memory_space=pltpu.SMEM),
            pl.BlockSpec((tb, ROWS_PER_EXPERT, LANES), lambda i: (i, 0, 0)),
            row, row,
            pl.BlockSpec(half_sum.shape, lambda i: (0, 0)),
            pl.BlockSpec(memory_space=pltpu.VMEM),
        ],
        out_specs=row,
        out_shape=jax.ShapeDtypeStruct((t, PEER_PAIRS), F32),
        scratch_shapes=[pltpu.VMEM((tb, PEER_PAIRS), F32), pltpu.VMEM((tb, PEER_PAIRS), F32)],
        compiler_params=pltpu.CompilerParams(
            dimension_semantics=("parallel",), vmem_limit_bytes=VMEM_LIMIT_TABLE),
        name="peer_u",
    )(tile, h3, par, gate, half_sum, tab)


def _peer_v_kernel(tile_ref, act_ref, par_ref, exp_ref, tab_ref, y_ref, coef_ref):
    tb = act_ref.shape[0]
    act = act_ref[...]
    hi = act.astype(BF16)
    lo = (act - hi.astype(F32)).astype(BF16)
    lhs = jnp.concatenate([hi, lo, par_ref[...].astype(BF16)], axis=0)
    coef_ref[...] = _dot(lhs, exp_ref[...])
    shape = (ROWS_PER_EXPERT, SLOT_ROWS)
    q_minus_r = (lax.broadcasted_iota(jnp.int32, shape, 1) % TILE_ROWS
                 - lax.broadcasted_iota(jnp.int32, shape, 0)).astype(F32)

    def consume(t, xs):
        def spread(kind, side):
            r0 = kind * tb + t
            return jnp.broadcast_to(coef_ref[r0:r0 + 1, side * SLOT_ROWS:(side + 1) * SLOT_ROWS], shape)

        parts = {}
        for side in range(2):
            mine = q_minus_r == spread(2, side) * float(ROWS_PER_EXPERT)
            for kind in range(2):
                parts[kind, side] = jnp.where(mine, spread(kind, side), 0.0)
        coef = jnp.concatenate([parts[0, 0], parts[0, 1], parts[1, 0], parts[1, 1]], axis=0).astype(BF16)
        res = _dot(coef, jnp.concatenate(xs, axis=0))
        n = ROWS_PER_EXPERT
        y_ref[t] = ((res[0:n, :LANES] + res[2 * n:3 * n, :LANES])
                    + (res[n:2 * n, LANES:] + res[3 * n:4 * n, LANES:]))

    xs = _gather_slots(tile_ref, tab_ref, 0)
    for t in range(tb):
        nxt = _gather_slots(tile_ref, tab_ref, t + 1) if t + 1 < tb else None
        consume(t, xs)
        xs = nxt


def _peer_v(tile, act, par, tab, tb):
    t = tile.shape[0]
    j = jnp.arange(PEER_PAIRS)[:, None]
    c = jnp.arange(2 * SLOT_ROWS)[None, :]
    expand = (2 * ((c % SLOT_ROWS) // TILE_ROWS) + c // SLOT_ROWS == j).astype(BF16)
    row = pl.BlockSpec((tb, PEER_PAIRS), lambda i: (i, 0))
    return pl.pallas_call(
        _peer_v_kernel,
        grid=(t // tb,),
        in_specs=[
            pl.BlockSpec((tb, PEER_PAIRS), lambda i: (i, 0), memory_space=pltpu.SMEM),
            row, row,
            pl.BlockSpec(expand.shape, lambda i: (0, 0)),
            pl.BlockSpec(memory_space=pltpu.VMEM),
        ],
        out_specs=pl.BlockSpec((tb, ROWS_PER_EXPERT, LANES), lambda i: (i, 0, 0)),
        out_shape=jax.ShapeDtypeStruct((t, ROWS_PER_EXPERT, LANES), F32),
        scratch_shapes=[pltpu.VMEM((3 * tb, 2 * SLOT_ROWS), F32)],
        compiler_params=pltpu.CompilerParams(
            dimension_semantics=("parallel",), vmem_limit_bytes=VMEM_LIMIT_TABLE),
        name="peer_v",
    )(tile, act, par, expand, tab)


def _ple_kernel(x1_ref, y_ref, p_ref, g_ref, wg_ref, wp_ref, o_ref):
    x2 = x1_ref[...] + y_ref[...]
    hg = _rms(x2, g_ref[...])
    gate = jax.nn.sigmoid(_dot(hg.astype(BF16), wg_ref[...]))
    o_ref[...] = x2 + gate * _dot(p_ref[...].astype(BF16), wp_ref[...])


def _ple(x1, y, p2d, g, wg, wp, tm):
    t, d = x1.shape
    row = lambda w: pl.BlockSpec((tm, w), lambda i: (i, 0))
    full = lambda a: pl.BlockSpec(a.shape, lambda i: (0,) * a.ndim)
    return pl.pallas_call(
        _ple_kernel,
        grid=(t // tm,),
        in_specs=[row(d), row(d), row(p2d.shape[1]), full(g), full(wg), full(wp)],
        out_specs=row(d),
        out_shape=jax.ShapeDtypeStruct((t, d), F32),
        compiler_params=pltpu.CompilerParams(
            dimension_semantics=("parallel",), vmem_limit_bytes=VMEM_LIMIT_DENSE),
        name="ple",
    )(x1, y, p2d, g, wg, wp)


def _expert_table(emb):
    n, d = emb.shape
    assert d == ROWS_PER_EXPERT * LANES and n % 2 == 0
    tiles = emb.astype(BF16).reshape(n // 2, WORD_ROWS, 2, LANES)
    words = lax.bitcast_convert_type(jnp.swapaxes(tiles, -1, -2), jnp.uint32)
    return words.reshape(n // 2 * WORD_ROWS, LANES)


def _layer(x, p, g_mix, w_in, q_norm, k_norm, sinks, w_out, g_ffn, w_query, sub_keys, emb_u, emb_v,
           g_ple, w_gate, w_proj):
    b, s, d = x.shape
    t = b * s
    tm = 256 if t % 256 == 0 else CHUNK
    tb = PEER_TOKEN_BLOCK
    assert t % tb == 0
    x2d = x.reshape(t, d)
    proj = _in_proj(x2d, g_mix[None, :], w_in.astype(BF16), tm)
    proj3 = proj.reshape(b, s, proj.shape[1])
    ro = _retention(proj3, _retention_tables(s))
    so = _swa(proj3, sinks, q_norm[None, :], k_norm[None, :])
    wo = w_out.astype(BF16)
    x1, h2, qp = _out_proj(x2d, ro.reshape(t, RET_WIDTH), so.reshape(t, SWA_Q_WIDTH),
                           wo[:RET_WIDTH], wo[RET_WIDTH:], g_ffn[None, :], w_query.astype(BF16), tm)
    tile, par, gate = _peer_topk(qp, sub_keys.astype(BF16), tm)
    h3 = h2.reshape(t, ROWS_PER_EXPERT, LANES)
    act = _peer_u(tile, h3, par, gate, _expert_table(emb_u), tb)
    y = _peer_v(tile, act, par, _expert_table(emb_v), tb)
    out = _ple(x1, y.reshape(t, d), p.reshape(t, p.shape[-1]), g_ple[None, :],
               w_gate.astype(BF16), w_proj.astype(BF16), tm)
    return out.reshape(b, s, d)


def kernel(x, p, g_mix, w_in, q_norm, k_norm, sinks, w_out, g_ffn, peer_w_query, peer_sub_keys,
           peer_u, peer_v, g_ple, w_ple_gate, w_ple_proj):
    for i in range(p.shape[0]):
        x = _layer(x, p[i], g_mix[i], w_in[i], q_norm[i], k_norm[i], sinks[i], w_out[i], g_ffn[i],
                   peer_w_query[i], peer_sub_keys[i], peer_u[i], peer_v[i], g_ple[i], w_ple_gate[i],
                   w_ple_proj[i])
    return x
```

```python
import functools

import jax
import jax.numpy as jnp
from jax import lax
from jax.experimental import pallas as pl
from jax.experimental.pallas import tpu as pltpu

F32 = jnp.float32
BF16 = jnp.bfloat16

HEAD_DIM = 64
RET_HEADS = 8
SWA_Q_HEADS = 8
SWA_KV_HEADS = 2
SWA_GROUP = SWA_Q_HEADS // SWA_KV_HEADS
RET_WIDTH = RET_HEADS * HEAD_DIM
SWA_Q_WIDTH = SWA_Q_HEADS * HEAD_DIM
SWA_KV_WIDTH = SWA_KV_HEADS * HEAD_DIM
CHUNK = 128
ROPE_BASE = 10000.0
PEER_HEADS = 8
PEER_KEYS = 128
PEER_HALF = 128
PEER_TOPK = 16
PEER_PAIRS = PEER_HEADS * PEER_TOPK
EPS = 1e-6

LANES = 128
F32_SUBLANES = 8
BF16_SUBLANES = 16
VMEM_LIMIT_DENSE = 48 * 1024 * 1024
VMEM_LIMIT_TABLE = 56 * 1024 * 1024

ROWS_PER_EXPERT = 1024 // LANES
TILE_ROWS = BF16_SUBLANES


def _rms(x, g):
    ms = jnp.mean(x * x, axis=-1, keepdims=True)
    return x * lax.rsqrt(ms + EPS) * g


def _dot(a, b):
    return jnp.dot(a, b, preferred_element_type=F32)


def _dot_nt(a, b):
    return lax.dot_general(a, b, (((1,), (1,)), ((), ())), preferred_element_type=F32)


def _dot_tn(a, b):
    return lax.dot_general(a, b, (((0,), (0,)), ((), ())), preferred_element_type=F32)


def _in_proj_kernel(x_ref, g_ref, w_ref, o_ref):
    h = _rms(x_ref[...], g_ref[...])
    o_ref[...] = _dot(h.astype(BF16), w_ref[...])


def _in_proj(x2d, g, w, tm):
    t, d = x2d.shape
    n = w.shape[1]
    return pl.pallas_call(
        _in_proj_kernel,
        grid=(t // tm,),
        in_specs=[
            pl.BlockSpec((tm, d), lambda i: (i, 0)),
            pl.BlockSpec((1, d), lambda i: (0, 0)),
            pl.BlockSpec((d, n), lambda i: (0, 0)),
        ],
        out_specs=pl.BlockSpec((tm, n), lambda i: (i, 0)),
        out_shape=jax.ShapeDtypeStruct((t, n), F32),
        compiler_params=pltpu.CompilerParams(
            dimension_semantics=("parallel",), vmem_limit_bytes=VMEM_LIMIT_DENSE),
        name="in_proj",
    )(x2d, g, w)


def _retention_kernel(cd_ref, q_ref, k_ref, v_ref, g_ref, cos_ref, sin_ref, dec_ref, wk_ref, wq_ref,
                      o_ref, state_ref):
    n = pl.program_id(1)

    @pl.when(n == 0)
    def _():
        state_ref[...] = jnp.zeros_like(state_ref)

    c = cos_ref[...]
    s = sin_ref[...]
    lane = lax.broadcasted_iota(jnp.int32, (CHUNK, RET_WIDTH), 1)
    first_half = (lane % HEAD_DIM) < (HEAD_DIM // 2)

    def rot(x):
        partner = jnp.where(first_half,
                            pltpu.roll(x, RET_WIDTH - HEAD_DIM // 2, 1),
                            pltpu.roll(x, HEAD_DIM // 2, 1))
        return x * c + partner * s

    q = rot(q_ref[...])
    k = rot(k_ref[...]) * (HEAD_DIM ** -0.5)
    kw = k * wk_ref[...]
    v = v_ref[...]
    gate = g_ref[...]
    wq = wq_ref[...]
    for h in range(RET_HEADS):
        sl = slice(h * HEAD_DIM, (h + 1) * HEAD_DIM)
        qh = q[:, sl].astype(BF16)
        kh = k[:, sl].astype(BF16)
        vh = v[:, sl].astype(BF16)
        att = _dot_nt(qh, kh) * dec_ref[h]
        o = _dot(att.astype(BF16), vh)
        st = state_ref[h]
        o = o + _dot(qh, st.astype(BF16)) * wq[:, sl]
        state_ref[h] = st * cd_ref[h] + _dot_tn(kw[:, sl].astype(BF16), vh)
        mu = jnp.mean(o, axis=-1, keepdims=True)
        oc = o - mu
        var = jnp.mean(oc * oc, axis=-1, keepdims=True)
        on = oc * lax.rsqrt(var + EPS)
        gh = gate[:, sl]
        o_ref[:, sl] = gh * jax.nn.sigmoid(gh) * on


def _retention(proj3, tables):
    b, s, _ = proj3.shape
    cd, cos_t, sin_t, dec, wk, wq = tables
    col = lambda j: pl.BlockSpec((None, CHUNK, RET_WIDTH), lambda bi, ni, j=j: (bi, ni, j))
    pos_spec = pl.BlockSpec((CHUNK, RET_WIDTH), lambda bi, ni: (ni, 0))
    const2 = pl.BlockSpec((CHUNK, RET_WIDTH), lambda bi, ni: (0, 0))
    return pl.pallas_call(
        _retention_kernel,
        grid=(b, s // CHUNK),
        in_specs=[
            pl.BlockSpec(memory_space=pltpu.SMEM),
            col(0), col(1), col(2), col(3),
            pos_spec, pos_spec,
            pl.BlockSpec((RET_HEADS, CHUNK, CHUNK), lambda bi, ni: (0, 0, 0)),
            const2, const2,
        ],
        out_specs=pl.BlockSpec((None, CHUNK, RET_WIDTH), lambda bi, ni: (bi, ni, 0)),
        out_shape=jax.ShapeDtypeStruct((b, s, RET_WIDTH), F32),
        scratch_shapes=[pltpu.VMEM((RET_HEADS, HEAD_DIM, HEAD_DIM), F32)],
        compiler_params=pltpu.CompilerParams(
            dimension_semantics=("parallel", "arbitrary"), vmem_limit_bytes=VMEM_LIMIT_DENSE),
        name="retention",
    )(cd, proj3, proj3, proj3, proj3, cos_t, sin_t, dec, wk, wq)


def _retention_tables(s):
    half = HEAD_DIM // 2
    pos = jnp.arange(s, dtype=F32)
    freqs = ROPE_BASE ** (-jnp.arange(half, dtype=F32) / half)
    ang = pos[:, None] * freqs[None, :]
    c = jnp.cos(ang)
    sn = jnp.sin(ang)
    cos_t = jnp.tile(jnp.concatenate([c, c], axis=-1), (1, RET_HEADS))
    sin_t = jnp.tile(jnp.concatenate([-sn, sn], axis=-1), (1, RET_HEADS))
    gamma = 1.0 - jnp.exp2(-5.0 - jnp.arange(RET_HEADS, dtype=F32))
    log_g = jnp.log(gamma)
    idx = jnp.arange(CHUNK)
    rel = idx[:, None] - idx[None, :]
    dec = jnp.where(rel[None] >= 0,
                    jnp.exp(log_g[:, None, None] * jnp.maximum(rel, 0)[None].astype(F32)),
                    0.0)
    w_k = jnp.exp(log_g[None, :] * (CHUNK - 1 - idx)[:, None].astype(F32))
    w_q = jnp.exp(log_g[None, :] * (idx + 1)[:, None].astype(F32))
    cd = jnp.exp(log_g * CHUNK)
    wk = jnp.repeat(w_k, HEAD_DIM, axis=1)
    wq = jnp.repeat(w_q, HEAD_DIM, axis=1)
    return cd, cos_t, sin_t, dec, wk, wq


def _swa_kernel(sink_ref, q_ref, kc_ref, vc_ref, kp_ref, vp_ref, qn_ref, kn_ref, o_ref):
    n = pl.program_id(1)
    qi = lax.broadcasted_iota(jnp.int32, (CHUNK, 2 * CHUNK), 0)
    ki = lax.broadcasted_iota(jnp.int32, (CHUNK, 2 * CHUNK), 1)
    rel = qi + CHUNK - ki
    mask = (rel >= 0) & (rel < CHUNK) & ((n > 0) | (ki >= CHUNK))
    q = q_ref[...]
    qn = qn_ref[...]
    kn = kn_ref[...]
    for g in range(SWA_KV_HEADS):
        sl = slice(g * HEAD_DIM, (g + 1) * HEAD_DIM)
        kcat = jnp.concatenate([kp_ref[:, sl], kc_ref[:, sl]], axis=0)
        kcat = _rms(kcat, kn).astype(BF16)
        vcat = jnp.concatenate([vp_ref[:, sl], vc_ref[:, sl]], axis=0).astype(BF16)
        for j in range(SWA_GROUP):
            h = g * SWA_GROUP + j
            hs = slice(h * HEAD_DIM, (h + 1) * HEAD_DIM)
            qh = _rms(q[:, hs], qn).astype(BF16)
            sc = _dot_nt(qh, kcat) * (HEAD_DIM ** -0.5)
            sc = jnp.where(mask, sc, -jnp.inf)
            sink = sink_ref[h]
            m = jnp.maximum(jnp.max(sc, axis=-1, keepdims=True), sink)
            e = jnp.exp(sc - m)
            pr = e / (jnp.sum(e, axis=-1, keepdims=True) + jnp.exp(sink - m))
            o_ref[:, hs] = _dot(pr.astype(BF16), vcat)


def _swa(proj3, sinks, qn, kn):
    b, s, _ = proj3.shape
    q_blk = (4 * RET_WIDTH) // SWA_Q_WIDTH
    k_blk = (4 * RET_WIDTH + SWA_Q_WIDTH) // SWA_KV_WIDTH
    v_blk = k_blk + 1
    cur = lambda j: pl.BlockSpec((None, CHUNK, SWA_KV_WIDTH), lambda bi, ni, j=j: (bi, ni, j))
    prev = lambda j: pl.BlockSpec((None, CHUNK, SWA_KV_WIDTH),
                                  lambda bi, ni, j=j: (bi, jnp.maximum(ni - 1, 0), j))
    return pl.pallas_call(
        _swa_kernel,
        grid=(b, s // CHUNK),
        in_specs=[
            pl.BlockSpec(memory_space=pltpu.SMEM),
            pl.BlockSpec((None, CHUNK, SWA_Q_WIDTH), lambda bi, ni: (bi, ni, q_blk)),
            cur(k_blk), cur(v_blk), prev(k_blk), prev(v_blk),
            pl.BlockSpec((1, HEAD_DIM), lambda bi, ni: (0, 0)),
            pl.BlockSpec((1, HEAD_DIM), lambda bi, ni: (0, 0)),
        ],
        out_specs=pl.BlockSpec((None, CHUNK, SWA_Q_WIDTH), lambda bi, ni: (bi, ni, 0)),
        out_shape=jax.ShapeDtypeStruct((b, s, SWA_Q_WIDTH), F32),
        compiler_params=pltpu.CompilerParams(
            dimension_semantics=("parallel", "parallel"), vmem_limit_bytes=VMEM_LIMIT_DENSE),
        name="swa",
    )(sinks, proj3, proj3, proj3, proj3, proj3, qn, kn)


def _out_proj_kernel(x_ref, ro_ref, so_ref, wo1_ref, wo2_ref, g_ref, wq_ref, x1_ref, h2_ref, qp_ref):
    x1 = (x_ref[...] + _dot(ro_ref[...].astype(BF16), wo1_ref[...])
          + _dot(so_ref[...].astype(BF16), wo2_ref[...]))
    x1_ref[...] = x1
    h2 = _rms(x1, g_ref[...])
    h2_ref[...] = h2
    qp = _dot(h2.astype(BF16), wq_ref[...])
    width = qp_ref.shape[2]
    for h in range(PEER_HEADS):
        qp_ref[h] = qp[:, h * width:(h + 1) * width]


def _out_proj(x2d, ro, so, wo1, wo2, g, wq, tm):
    t, d = x2d.shape
    nq = wq.shape[1]
    width = nq // PEER_HEADS
    row = lambda w: pl.BlockSpec((tm, w), lambda i: (i, 0))
    full = lambda a: pl.BlockSpec(a.shape, lambda i: (0,) * a.ndim)
    return pl.pallas_call(
        _out_proj_kernel,
        grid=(t // tm,),
        in_specs=[row(d), row(ro.shape[1]), row(so.shape[1]), full(wo1), full(wo2), full(g), full(wq)],
        out_specs=[row(d), row(d), pl.BlockSpec((PEER_HEADS, tm, width), lambda i: (0, i, 0))],
        out_shape=[jax.ShapeDtypeStruct((t, d), F32), jax.ShapeDtypeStruct((t, d), F32),
                   jax.ShapeDtypeStruct((PEER_HEADS, t, width), F32)],
        compiler_params=pltpu.CompilerParams(
            dimension_semantics=("parallel",), vmem_limit_bytes=VMEM_LIMIT_DENSE),
        name="out_proj",
    )(x2d, ro, so, wo1, wo2, g, wq)


_CAND_B = (16, 8, 8, 8, 8, 8, 8, 8)


def _candidate_flat_index():
    pos = [a * PEER_TOPK + b for a, nb in enumerate(_CAND_B) for b in range(nb)]
    pos += [a * PEER_TOPK for a in range(len(_CAND_B), PEER_TOPK)]
    return pos


def _peer_topk_kernel(qp_ref, keys_ref, pos_ref, tile_ref, par_ref, gate_ref,
                      s_ref, i_ref, best_ref, exp_ref, et_ref, gt_ref):
    h = pl.program_id(1)
    tm = qp_ref.shape[0]
    neg = -jnp.inf
    key_id = lax.broadcasted_iota(jnp.int32, (PEER_KEYS, tm), 0).astype(F32)
    for p in range(2):
        qhp = qp_ref[:, p * PEER_HALF:(p + 1) * PEER_HALF].astype(BF16)
        sc = _dot_nt(keys_ref[p], qhp)
        for a in range(PEER_TOPK):
            m = jnp.max(sc, axis=0, keepdims=True)
            i = jnp.min(jnp.where(sc == m, key_id, float(PEER_KEYS)), axis=0, keepdims=True)
            sc = jnp.where(key_id == i, neg, sc)
            s_ref[p, a:a + 1, :] = m
            i_ref[p, a:a + 1, :] = i
    s0, s1 = s_ref[0], s_ref[1]
    i0, i1 = i_ref[0] * float(PEER_KEYS), i_ref[1]
    cs_blocks, ci_blocks = [], []
    for a, nb in enumerate(_CAND_B):
        cs_blocks.append(s0[a:a + 1] + s1[0:nb])
        ci_blocks.append(i0[a:a + 1] + i1[0:nb])
    na = len(_CAND_B)
    cs_blocks.append(s0[na:] + s1[0:1])
    ci_blocks.append(i0[na:] + i1[0:1])
    cs = jnp.concatenate(cs_blocks, axis=0)
    ci = jnp.concatenate(ci_blocks, axis=0)
    pos = pos_ref[...]
    for k in range(PEER_TOPK):
        m = jnp.max(cs, axis=0, keepdims=True)
        first = jnp.min(jnp.where(cs == m, pos, float(PEER_TOPK * PEER_TOPK)), axis=0, keepdims=True)
        sel = pos == first
        best_ref[k:k + 1, :] = m
        exp_ref[k:k + 1, :] = jnp.max(jnp.where(sel, ci, -1.0), axis=0, keepdims=True)
        cs = jnp.where(sel, neg, cs)
    best = best_ref[...]
    ex = jnp.exp(best - best[0:1])
    gate = ex / jnp.sum(ex, axis=0, keepdims=True)
    rows = pl.ds(pl.multiple_of(h * PEER_TOPK, PEER_TOPK), PEER_TOPK)
    et_ref[rows, :] = exp_ref[...]
    gt_ref[rows, :] = gate

    @pl.when(h == pl.num_programs(1) - 1)
    def _():
        expert = et_ref[...].T.astype(jnp.int32)
        tile_ref[...] = (expert >> 1) * WORD_ROWS
        par_ref[...] = (expert & 1).astype(F32)
        gate_ref[...] = gt_ref[...].T


def _peer_topk(qp, keys, tm):
    nh, t, width = qp.shape
    pos = _candidate_flat_index()
    pos = jnp.broadcast_to(jnp.asarray(pos, F32)[:, None], (len(pos), tm))
    full = lambda a: pl.BlockSpec(a.shape, lambda i, h: (0,) * a.ndim)
    out_spec = pl.BlockSpec((tm, PEER_PAIRS), lambda i, h: (i, 0))
    return pl.pallas_call(
        _peer_topk_kernel,
        grid=(t // tm, nh),
        in_specs=[pl.BlockSpec((None, tm, width), lambda i, h: (h, i, 0)), full(keys), full(pos)],
        out_specs=[out_spec, out_spec, out_spec],
        out_shape=[jax.ShapeDtypeStruct((t, PEER_PAIRS), jnp.int32),
                   jax.ShapeDtypeStruct((t, PEER_PAIRS), F32),
                   jax.ShapeDtypeStruct((t, PEER_PAIRS), F32)],
        scratch_shapes=[pltpu.VMEM((2, PEER_TOPK, tm), F32), pltpu.VMEM((2, PEER_TOPK, tm), F32),
                        pltpu.VMEM((PEER_TOPK, tm), F32), pltpu.VMEM((PEER_TOPK, tm), F32),
                        pltpu.VMEM((PEER_PAIRS, tm), F32), pltpu.VMEM((PEER_PAIRS, tm), F32)],
        compiler_params=pltpu.CompilerParams(
            dimension_semantics=("parallel", "arbitrary"), vmem_limit_bytes=VMEM_LIMIT_DENSE),
        name="peer_topk",
    )(qp, keys, pos)


WORD_ROWS = TILE_ROWS // 2
SLOTS = PEER_PAIRS // 2
SLOT_ROWS = SLOTS * TILE_ROWS
MXU_ROWS = 256
SLOTS_PER_MXU_TILE = MXU_ROWS // TILE_ROWS
PEER_TOKEN_BLOCK = 16


def _gather_slots(tile_ref, tab_ref, t, scale=None):
    def tile(j):
        start = pl.multiple_of(tile_ref[t, j], WORD_ROWS)
        x = pltpu.bitcast(tab_ref[pl.ds(start, WORD_ROWS), :], BF16)
        return x if scale is None else x * scale

    blocks = []
    for k in range(SLOTS // SLOTS_PER_MXU_TILE):
        slots = [jnp.concatenate([tile(2 * i), tile(2 * i + 1)], axis=1)
                 for i in range(k * SLOTS_PER_MXU_TILE, (k + 1) * SLOTS_PER_MXU_TILE)]
        blocks.append(jnp.concatenate(slots, axis=0))
    return blocks


def _peer_u_kernel(tile_ref, h_ref, par_ref, gate_ref, half_ref, tab_ref, act_ref, za_ref, zb_ref):
    tb = h_ref.shape[0]
    ones = jnp.ones((BF16_SUBLANES, LANES), BF16)
    half_sum = half_ref[...]

    def gather(t):
        ht = h_ref[t].astype(BF16)
        return _gather_slots(tile_ref, tab_ref, t, scale=jnp.concatenate([ht, ht], axis=0))

    def half_sums(prod):
        z = [_dot(half_sum, block) for block in prod]
        return jnp.concatenate(z, axis=0).astype(BF16)

    def lane_sums(t, z):
        s = _dot_nt(ones, jnp.concatenate([z[:, :LANES], z[:, LANES:]], axis=0))
        za_ref[t:t + 1, :] = s[0:1, :PEER_PAIRS]
        zb_ref[t:t + 1, :] = s[0:1, PEER_PAIRS:]

    prod = gather(0)
    pending = None
    for t in range(tb):
        nxt = gather(t + 1) if t + 1 < tb else None
        z = half_sums(prod)
        if pending is not None:
            lane_sums(t - 1, pending)
        pending, prod = z, nxt
    lane_sums(tb - 1, pending)
    za = za_ref[...]
    zb = zb_ref[...]
    first = par_ref[...] == 0.0
    even_pair = jnp.where(first, za, pltpu.roll(za, LANES - 1, 1))
    odd_pair = jnp.where(first, pltpu.roll(zb, 1, 1), zb)
    lane = lax.broadcasted_iota(jnp.int32, za.shape, 1)
    a = jnp.where(lane % 2 == 0, even_pair, odd_pair)
    act_ref[...] = 0.5 * a * (1.0 + lax.erf(a * (2.0 ** -0.5))) * gate_ref[...]


def _peer_u(tile, h3, par, gate, tab, tb):
    t = tile.shape[0]
    r = jnp.arange(2 * SLOTS_PER_MXU_TILE)[:, None]
    c = jnp.arange(MXU_ROWS)[None, :]
    half_sum = ((c // TILE_ROWS == r // 2) & ((c % TILE_ROWS) // ROWS_PER_EXPERT == r % 2)).astype(BF16)
    row = pl.BlockSpec((tb, PEER_PAIRS), lambda i: (i, 0))
    return pl.pallas_call(
        _peer_u_kernel,
        grid=(t // tb,),
        in_specs=[
            pl.BlockSpec((tb, PEER_PAIRS), lambda i: (i, 0), memory_space=pltpu.SMEM),
            pl.BlockSpec((tb, ROWS_PER_EXPERT, LANES), lambda i: (i, 0, 0)),
            row, row,
            pl.BlockSpec(half_sum.shape, lambda i: (0, 0)),
            pl.BlockSpec(memory_space=pltpu.VMEM),
        ],
        out_specs=row,
        out_shape=jax.ShapeDtypeStruct((t, PEER_PAIRS), F32),
        scratch_shapes=[pltpu.VMEM((tb, PEER_PAIRS), F32), pltpu.VMEM((tb, PEER_PAIRS), F32)],
        compiler_params=pltpu.CompilerParams(
            dimension_semantics=("parallel",), vmem_limit_bytes=VMEM_LIMIT_TABLE),
        name="peer_u",
    )(tile, h3, par, gate, half_sum, tab)


def _peer_v_kernel(tile_ref, act_ref, par_ref, exp_ref, tab_ref, y_ref, coef_ref):
    tb = act_ref.shape[0]
    n = ROWS_PER_EXPERT
    act = act_ref[...]
    odd = par_ref[...]
    parts = []
    for a in (act * (1.0 - odd), act * odd):
        hi = a.astype(BF16)
        parts.append((hi, (a - hi.astype(F32)).astype(BF16)))
    lhs = jnp.concatenate([jnp.concatenate([parts[0][k], parts[1][k]], axis=1) for k in range(2)], axis=0)
    lhs = jnp.broadcast_to(lhs[:, None, :], (2 * tb, n, 2 * PEER_PAIRS)).reshape(2 * tb * n, 2 * PEER_PAIRS)
    coef_ref[...] = _dot(lhs, exp_ref[...])
    shape = (n, SLOT_ROWS)
    own_row = (lax.broadcasted_iota(jnp.int32, shape, 1) % n) == lax.broadcasted_iota(jnp.int32, shape, 0)

    def consume(t, xs):
        def part(kind, side):
            r0 = (kind * tb + t) * n
            return jnp.where(own_row, coef_ref[r0:r0 + n, side * SLOT_ROWS:(side + 1) * SLOT_ROWS], 0.0)

        coef = jnp.concatenate([part(0, 0), part(0, 1), part(1, 0), part(1, 1)], axis=0).astype(BF16)
        res = _dot(coef, jnp.concatenate(xs, axis=0))
        y_ref[t] =((res[0:n, :LANES] + res[2 * n:3 * n, :LANES])
                    + (res[n:2 * n, LANES:] + res[3 * n:4 * n, LANES:]))

    xs = _gather_slots(tile_ref, tab_ref, 0)
    for t in range(tb):
        nxt = _gather_slots(tile_ref, tab_ref, t + 1) if t + 1 < tb else None
        consume(t, xs)
        xs = nxt


def _peer_v(tile, act, par, tab, tb):
    t = tile.shape[0]
    k = jnp.arange(2 * PEER_PAIRS)[:, None]
    c = jnp.arange(2 * SLOT_ROWS)[None, :]
    pair_of_col = 2 * ((c % SLOT_ROWS) // TILE_ROWS) + c // SLOT_ROWS
    half_of_col = (c % TILE_ROWS) // ROWS_PER_EXPERT
    expand = ((pair_of_col == k % PEER_PAIRS) & (half_of_col == k // PEER_PAIRS)).astype(BF16)
    row = pl.BlockSpec((tb, PEER_PAIRS), lambda i: (i, 0))
    return pl.pallas_call(
        _peer_v_kernel,
        grid=(t // tb,),
        in_specs=[
            pl.BlockSpec((tb, PEER_PAIRS), lambda i: (i, 0), memory_space=pltpu.SMEM),
            row, row,
            pl.BlockSpec(expand.shape, lambda i: (0, 0)),
            pl.BlockSpec(memory_space=pltpu.VMEM),
        ],
        out_specs=pl.BlockSpec((tb, ROWS_PER_EXPERT, LANES), lambda i: (i, 0, 0)),
        out_shape=jax.ShapeDtypeStruct((t, ROWS_PER_EXPERT, LANES), F32),
        scratch_shapes=[pltpu.VMEM((2 * tb * ROWS_PER_EXPERT, 2 * SLOT_ROWS), F32)],
        compiler_params=pltpu.CompilerParams(
            dimension_semantics=("parallel",), vmem_limit_bytes=VMEM_LIMIT_TABLE),
        name="peer_v",
    )(tile, act, par, expand, tab)


def _ple_kernel(x1_ref, y_ref, p_ref, g_ref, wg_ref, wp_ref, o_ref):
    x2 = x1_ref[...] + y_ref[...]
    hg = _rms(x2, g_ref[...])
    gate = jax.nn.sigmoid(_dot(hg.astype(BF16), wg_ref[...]))
    o_ref[...] = x2 + gate * _dot(p_ref[...].astype(BF16), wp_ref[...])


def _ple(x1, y, p2d, g, wg, wp, tm):
    t, d = x1.shape
    row = lambda w: pl.BlockSpec((tm, w), lambda i: (i, 0))
    full = lambda a: pl.BlockSpec(a.shape, lambda i: (0,) * a.ndim)
    return pl.pallas_call(
        _ple_kernel,
        grid=(t // tm,),
        in_specs=[row(d), row(d), row(p2d.shape[1]), full(g), full(wg), full(wp)],
        out_specs=row(d),
        out_shape=jax.ShapeDtypeStruct((t, d), F32),
        compiler_params=pltpu.CompilerParams(
            dimension_semantics=("parallel",), vmem_limit_bytes=VMEM_LIMIT_DENSE),
        name="ple",
    )(x1, y, p2d, g, wg, wp)


def _expert_table(emb):
    n, d = emb.shape
    assert d == ROWS_PER_EXPERT * LANES and n % 2 == 0
    tiles = emb.astype(BF16).reshape(n // 2, WORD_ROWS, 2, LANES)
    words = lax.bitcast_convert_type(jnp.swapaxes(tiles, -1, -2), jnp.uint32)
    return words.reshape(n // 2 * WORD_ROWS, LANES)


def _layer(x, p, g_mix, w_in, q_norm, k_norm, sinks, w_out, g_ffn, w_query, sub_keys, emb_u, emb_v,
           g_ple, w_gate, w_proj):
    b, s, d = x.shape
    t = b * s
    tm = 256 if t % 256 == 0 else CHUNK
    tb = PEER_TOKEN_BLOCK
    assert t % tb == 0
    x2d = x.reshape(t, d)
    proj = _in_proj(x2d, g_mix[None, :], w_in.astype(BF16), tm)
    proj3 = proj.reshape(b, s, proj.shape[1])
    ro = _retention(proj3, _retention_tables(s))
    so = _swa(proj3, sinks, q_norm[None, :], k_norm[None, :])
    wo = w_out.astype(BF16)
    x1, h2, qp = _out_proj(x2d, ro.reshape(t, RET_WIDTH), so.reshape(t, SWA_Q_WIDTH),
                           wo[:RET_WIDTH], wo[RET_WIDTH:], g_ffn[None, :], w_query.astype(BF16), tm)
    tile, par, gate = _peer_topk(qp, sub_keys.astype(BF16), tm)
    h3 = h2.reshape(t, ROWS_PER_EXPERT, LANES)
    act = _peer_u(tile, h3, par, gate, _expert_table(emb_u), tb)
    y = _peer_v(tile, act, par, _expert_table(emb_v), tb)
    out = _ple(x1, y.reshape(t, d), p.reshape(t, p.shape[-1]), g_ple[None, :],
               w_gate.astype(BF16), w_proj.astype(BF16), tm)
    return out.reshape(b, s, d)


def kernel(x, p, g_mix, w_in, q_norm, k_norm, sinks, w_out, g_ffn, peer_w_query, peer_sub_keys,
           peer_u, peer_v, g_ple, w_ple_gate, w_ple_proj):
    for i in range(p.shape[0]):
        x = _layer(x, p[i], g_mix[i], w_in[i], q_norm[i], k_norm[i], sinks[i], w_out[i], g_ffn[i],
                   peer_w_query[i], peer_sub_keys[i], peer_u[i], peer_v[i], g_ple[i], w_ple_gate[i],
                   w_ple_proj[i])
    return x
```

```python
import functools

import jax
import jax.numpy as jnp
from jax import lax
from jax.experimental import pallas as pl
from jax.experimental.pallas import tpu as pltpu

F32 = jnp.float32
BF16 = jnp.bfloat16

HEAD_DIM = 64
RET_HEADS = 8
SWA_Q_HEADS = 8
SWA_KV_HEADS = 2
SWA_GROUP = SWA_Q_HEADS // SWA_KV_HEADS
RET_WIDTH = RET_HEADS * HEAD_DIM
SWA_Q_WIDTH = SWA_Q_HEADS * HEAD_DIM
SWA_KV_WIDTH = SWA_KV_HEADS * HEAD_DIM
CHUNK = 128
ROPE_BASE = 10000.0
PEER_HEADS = 8
PEER_KEYS = 128
PEER_HALF = 128
PEER_TOPK = 16
PEER_PAIRS = PEER_HEADS * PEER_TOPK
EPS = 1e-6

LANES = 128
F32_SUBLANES = 8
BF16_SUBLANES = 16
VMEM_LIMIT_DENSE = 48 * 1024 * 1024
VMEM_LIMIT_TABLE = 56 * 1024 * 1024

ROWS_PER_EXPERT = 1024 // LANES
TILE_ROWS = BF16_SUBLANES


def _rms(x, g):
    ms = jnp.mean(x * x, axis=-1, keepdims=True)
    return x * lax.rsqrt(ms + EPS) * g


def _dot(a, b):
    return jnp.dot(a, b, preferred_element_type=F32)


def _dot_nt(a, b):
    return lax.dot_general(a, b, (((1,), (1,)), ((), ())), preferred_element_type=F32)


def _dot_tn(a, b):
    return lax.dot_general(a, b, (((0,), (0,)), ((), ())), preferred_element_type=F32)


def _in_proj_kernel(x_ref, g_ref, w_ref, o_ref):
    h = _rms(x_ref[...], g_ref[...])
    o_ref[...] = _dot(h.astype(BF16), w_ref[...])


def _in_proj(x2d, g, w, tm):
    t, d = x2d.shape
    n = w.shape[1]
    return pl.pallas_call(
        _in_proj_kernel,
        grid=(t // tm,),
        in_specs=[
            pl.BlockSpec((tm, d), lambda i: (i, 0)),
            pl.BlockSpec((1, d), lambda i: (0, 0)),
            pl.BlockSpec((d, n), lambda i: (0, 0)),
        ],
        out_specs=pl.BlockSpec((tm, n), lambda i: (i, 0)),
        out_shape=jax.ShapeDtypeStruct((t, n), F32),
        compiler_params=pltpu.CompilerParams(
            dimension_semantics=("parallel",), vmem_limit_bytes=VMEM_LIMIT_DENSE),
        name="in_proj",
    )(x2d, g, w)


def _retention_kernel(cd_ref, q_ref, k_ref, v_ref, g_ref, cos_ref, sin_ref, dec_ref, wk_ref, wq_ref,
                      o_ref, state_ref):
    n = pl.program_id(1)

    @pl.when(n == 0)
    def _():
        state_ref[...] = jnp.zeros_like(state_ref)

    c = cos_ref[...]
    s = sin_ref[...]
    lane = lax.broadcasted_iota(jnp.int32, (CHUNK, RET_WIDTH), 1)
    first_half = (lane % HEAD_DIM) < (HEAD_DIM // 2)

    def rot(x):
        partner = jnp.where(first_half,
                            pltpu.roll(x, RET_WIDTH - HEAD_DIM // 2, 1),
                            pltpu.roll(x, HEAD_DIM // 2, 1))
        return x * c + partner * s

    q = rot(q_ref[...])
    k = rot(k_ref[...]) * (HEAD_DIM ** -0.5)
    kw = k * wk_ref[...]
    v = v_ref[...]
    gate = g_ref[...]
    wq = wq_ref[...]
    for h in range(RET_HEADS):
        sl = slice(h * HEAD_DIM, (h + 1) * HEAD_DIM)
        qh = q[:, sl].astype(BF16)
        kh = k[:, sl].astype(BF16)
        vh = v[:, sl].astype(BF16)
        att = _dot_nt(qh, kh) * dec_ref[h]
        o = _dot(att.astype(BF16), vh)
        st = state_ref[h]
        o = o + _dot(qh, st.astype(BF16)) * wq[:, sl]
        state_ref[h] = st * cd_ref[h] + _dot_tn(kw[:, sl].astype(BF16), vh)
        mu = jnp.mean(o, axis=-1, keepdims=True)
        oc = o - mu
        var = jnp.mean(oc * oc, axis=-1, keepdims=True)
        on = oc * lax.rsqrt(var + EPS)
        gh = gate[:, sl]
        o_ref[:, sl] = gh * jax.nn.sigmoid(gh) * on


def _retention(proj3, tables):
    b, s, _ = proj3.shape
    cd, cos_t, sin_t, dec, wk, wq = tables
    col = lambda j: pl.BlockSpec((None, CHUNK, RET_WIDTH), lambda bi, ni, j=j: (bi, ni, j))
    pos_spec = pl.BlockSpec((CHUNK, RET_WIDTH), lambda bi, ni: (ni, 0))
    const2 = pl.BlockSpec((CHUNK, RET_WIDTH), lambda bi, ni: (0, 0))
    return pl.pallas_call(
        _retention_kernel,
        grid=(b, s // CHUNK),
        in_specs=[
            pl.BlockSpec(memory_space=pltpu.SMEM),
            col(0), col(1), col(2), col(3),
            pos_spec, pos_spec,
            pl.BlockSpec((RET_HEADS, CHUNK, CHUNK), lambda bi, ni: (0, 0, 0)),
            const2, const2,
        ],
        out_specs=pl.BlockSpec((None, CHUNK, RET_WIDTH), lambda bi, ni: (bi, ni, 0)),
        out_shape=jax.ShapeDtypeStruct((b, s, RET_WIDTH), F32),
        scratch_shapes=[pltpu.VMEM((RET_HEADS, HEAD_DIM, HEAD_DIM), F32)],
        compiler_params=pltpu.CompilerParams(
            dimension_semantics=("parallel", "arbitrary"), vmem_limit_bytes=VMEM_LIMIT_DENSE),
        name="retention",
    )(cd, proj3, proj3, proj3, proj3, cos_t, sin_t, dec, wk, wq)


def _retention_tables(s):
    half = HEAD_DIM // 2
    pos = jnp.arange(s, dtype=F32)
    freqs = ROPE_BASE ** (-jnp.arange(half, dtype=F32) / half)
    ang = pos[:, None] * freqs[None, :]
    c = jnp.cos(ang)
    sn = jnp.sin(ang)
    cos_t = jnp.tile(jnp.concatenate([c, c], axis=-1), (1, RET_HEADS))
    sin_t = jnp.tile(jnp.concatenate([-sn, sn], axis=-1), (1, RET_HEADS))
    gamma = 1.0 - jnp.exp2(-5.0 - jnp.arange(RET_HEADS, dtype=F32))
    log_g = jnp.log(gamma)
    idx = jnp.arange(CHUNK)
    rel = idx[:, None] - idx[None, :]
    dec = jnp.where(rel[None] >= 0,
                    jnp.exp(log_g[:, None, None] * jnp.maximum(rel, 0)[None].astype(F32)),
                    0.0)
    w_k = jnp.exp(log_g[None, :] * (CHUNK - 1 - idx)[:, None].astype(F32))
    w_q = jnp.exp(log_g[None, :] * (idx + 1)[:, None].astype(F32))
    cd = jnp.exp(log_g * CHUNK)
    wk = jnp.repeat(w_k, HEAD_DIM, axis=1)
    wq = jnp.repeat(w_q, HEAD_DIM, axis=1)
    return cd, cos_t, sin_t, dec, wk, wq


def _swa_kernel(sink_ref, q_ref, kc_ref, vc_ref, kp_ref, vp_ref, qn_ref, kn_ref, o_ref):
    n = pl.program_id(1)
    qi = lax.broadcasted_iota(jnp.int32, (CHUNK, 2 * CHUNK), 0)
    ki = lax.broadcasted_iota(jnp.int32, (CHUNK, 2 * CHUNK), 1)
    rel = qi + CHUNK - ki
    mask = (rel >= 0) & (rel < CHUNK) & ((n > 0) | (ki >= CHUNK))
    q = q_ref[...]
    qn = qn_ref[...]
    kn = kn_ref[...]
    for g in range(SWA_KV_HEADS):
        sl = slice(g * HEAD_DIM, (g + 1) * HEAD_DIM)
        kcat = jnp.concatenate([kp_ref[:, sl], kc_ref[:, sl]], axis=0)
        kcat = _rms(kcat, kn).astype(BF16)
        vcat = jnp.concatenate([vp_ref[:, sl], vc_ref[:, sl]], axis=0).astype(BF16)
        for j in range(SWA_GROUP):
            h = g * SWA_GROUP + j
            hs = slice(h * HEAD_DIM, (h + 1) * HEAD_DIM)
            qh = _rms(q[:, hs], qn).astype(BF16)
            sc = _dot_nt(qh, kcat) * (HEAD_DIM ** -0.5)
            sc = jnp.where(mask, sc, -jnp.inf)
            sink = sink_ref[h]
            m = jnp.maximum(jnp.max(sc, axis=-1, keepdims=True), sink)
            e = jnp.exp(sc - m)
            pr = e / (jnp.sum(e, axis=-1, keepdims=True) + jnp.exp(sink - m))
            o_ref[:, hs] = _dot(pr.astype(BF16), vcat)


def _swa(proj3, sinks, qn, kn):
    b, s, _ = proj3.shape
    q_blk = (4 * RET_WIDTH) // SWA_Q_WIDTH
    k_blk = (4 * RET_WIDTH + SWA_Q_WIDTH) // SWA_KV_WIDTH
    v_blk = k_blk + 1
    cur = lambda j: pl.BlockSpec((None, CHUNK, SWA_KV_WIDTH), lambda bi, ni, j=j: (bi, ni, j))
    prev = lambda j: pl.BlockSpec((None, CHUNK, SWA_KV_WIDTH),
                                  lambda bi, ni, j=j: (bi, jnp.maximum(ni - 1, 0), j))
    return pl.pallas_call(
        _swa_kernel,
        grid=(b, s // CHUNK),
        in_specs=[
            pl.BlockSpec(memory_space=pltpu.SMEM),
            pl.BlockSpec((None, CHUNK, SWA_Q_WIDTH), lambda bi, ni: (bi, ni, q_blk)),
            cur(k_blk), cur(v_blk), prev(k_blk), prev(v_blk),
            pl.BlockSpec((1, HEAD_DIM), lambda bi, ni: (0, 0)),
            pl.BlockSpec((1, HEAD_DIM), lambda bi, ni: (0, 0)),
        ],
        out_specs=pl.BlockSpec((None, CHUNK, SWA_Q_WIDTH), lambda bi, ni: (bi, ni, 0)),
        out_shape=jax.ShapeDtypeStruct((b, s, SWA_Q_WIDTH), F32),
        compiler_params=pltpu.CompilerParams(
            dimension_semantics=("parallel", "parallel"), vmem_limit_bytes=VMEM_LIMIT_DENSE),
        name="swa",
    )(sinks, proj3, proj3, proj3, proj3, proj3, qn, kn)


def _out_proj_kernel(x_ref, ro_ref, so_ref, wo1_ref, wo2_ref, g_ref, wq_ref, x1_ref, h2_ref, qp_ref):
    x1 = (x_ref[...] + _dot(ro_ref[...].astype(BF16), wo1_ref[...])
          + _dot(so_ref[...].astype(BF16), wo2_ref[...]))
    x1_ref[...] = x1
    h2 = _rms(x1, g_ref[...])
    for r in range(ROWS_PER_EXPERT):
        h2_ref[:, r, :] = h2[:, r * LANES:(r + 1) * LANES]
    qp = _dot(h2.astype(BF16), wq_ref[...])
    width = qp_ref.shape[2]
    for h in range(PEER_HEADS):
        qp_ref[h] = qp[:, h * width:(h + 1) * width]


def _out_proj(x2d, ro, so, wo1, wo2, g, wq, tm):
    t, d = x2d.shape
    nq = wq.shape[1]
    width = nq // PEER_HEADS
    row = lambda w: pl.BlockSpec((tm, w), lambda i: (i, 0))
    full = lambda a: pl.BlockSpec(a.shape, lambda i: (0,) * a.ndim)
    return pl.pallas_call(
        _out_proj_kernel,
        grid=(t // tm,),
        in_specs=[row(d), row(ro.shape[1]), row(so.shape[1]), full(wo1), full(wo2), full(g), full(wq)],
        out_specs=[row(d), pl.BlockSpec((tm, ROWS_PER_EXPERT, LANES), lambda i: (i, 0, 0)),
                   pl.BlockSpec((PEER_HEADS, tm, width), lambda i: (0, i, 0))],
        out_shape=[jax.ShapeDtypeStruct((t, d), F32), jax.ShapeDtypeStruct((t, ROWS_PER_EXPERT, LANES), F32),
                   jax.ShapeDtypeStruct((PEER_HEADS, t, width), F32)],
        compiler_params=pltpu.CompilerParams(
            dimension_semantics=("parallel",), vmem_limit_bytes=VMEM_LIMIT_DENSE),
        name="out_proj",
    )(x2d, ro, so, wo1, wo2, g, wq)


_CAND_B = (16, 8, 8, 8, 8, 8, 8, 8)


def _candidate_flat_index():
    pos = [a * PEER_TOPK + b for a, nb in enumerate(_CAND_B) for b in range(nb)]
    pos += [a * PEER_TOPK for a in range(len(_CAND_B), PEER_TOPK)]
    return pos


def _peer_topk_kernel(qp_ref, keys_ref, pos_ref, tile_ref, par_ref, gate_ref,
                      s_ref, i_ref, best_ref, exp_ref, et_ref, gt_ref):
    h = pl.program_id(1)
    tm = qp_ref.shape[0]
    neg = -jnp.inf
    key_id = lax.broadcasted_iota(jnp.int32, (PEER_KEYS, tm), 0).astype(F32)
    for p in range(2):
        qhp = qp_ref[:, p * PEER_HALF:(p + 1) * PEER_HALF].astype(BF16)
        sc = _dot_nt(keys_ref[p], qhp)
        for a in range(PEER_TOPK):
            m = jnp.max(sc, axis=0, keepdims=True)
            i = jnp.min(jnp.where(sc == m, key_id, float(PEER_KEYS)), axis=0, keepdims=True)
            sc = jnp.where(key_id == i, neg, sc)
            s_ref[p, a:a + 1, :] = m
            i_ref[p, a:a + 1, :] = i
    s0, s1 = s_ref[0], s_ref[1]
    i0, i1 = i_ref[0] * float(PEER_KEYS), i_ref[1]
    cs_blocks, ci_blocks = [], []
    for a, nb in enumerate(_CAND_B):
        cs_blocks.append(s0[a:a + 1] + s1[0:nb])
        ci_blocks.append(i0[a:a + 1] + i1[0:nb])
    na = len(_CAND_B)
    cs_blocks.append(s0[na:] + s1[0:1])
    ci_blocks.append(i0[na:] + i1[0:1])
    cs = jnp.concatenate(cs_blocks, axis=0)
    ci = jnp.concatenate(ci_blocks, axis=0)
    pos = pos_ref[...]
    for k in range(PEER_TOPK):
        m = jnp.max(cs, axis=0, keepdims=True)
        first = jnp.min(jnp.where(cs == m, pos, float(PEER_TOPK * PEER_TOPK)), axis=0, keepdims=True)
        sel = pos == first
        best_ref[k:k + 1, :] = m
        exp_ref[k:k + 1, :] = jnp.max(jnp.where(sel, ci, -1.0), axis=0, keepdims=True)
        cs = jnp.where(sel, neg, cs)
    best = best_ref[...]
    ex = jnp.exp(best - best[0:1])
    gate = ex / jnp.sum(ex, axis=0, keepdims=True)
    rows = pl.ds(pl.multiple_of(h * PEER_TOPK, PEER_TOPK), PEER_TOPK)
    et_ref[rows, :] = exp_ref[...]
    gt_ref[rows, :] = gate

    @pl.when(h == pl.num_programs(1) - 1)
    def _():
        expert = et_ref[...].T.astype(jnp.int32)
        tile_ref[...] = (expert >> 1) * WORD_ROWS
        par_ref[...] = (expert & 1).astype(F32)
        gate_ref[...] = gt_ref[...].T


def _peer_topk(qp, keys, tm):
    nh, t, width = qp.shape
    pos = _candidate_flat_index()
    pos = jnp.broadcast_to(jnp.asarray(pos, F32)[:, None], (len(pos), tm))
    full = lambda a: pl.BlockSpec(a.shape, lambda i, h: (0,) * a.ndim)
    out_spec = pl.BlockSpec((tm, PEER_PAIRS), lambda i, h: (i, 0))
    return pl.pallas_call(
        _peer_topk_kernel,
        grid=(t // tm, nh),
        in_specs=[pl.BlockSpec((None, tm, width), lambda i, h: (h, i, 0)), full(keys), full(pos)],
        out_specs=[out_spec, out_spec, out_spec],
        out_shape=[jax.ShapeDtypeStruct((t, PEER_PAIRS), jnp.int32),
                   jax.ShapeDtypeStruct((t, PEER_PAIRS), F32),
                   jax.ShapeDtypeStruct((t, PEER_PAIRS), F32)],
        scratch_shapes=[pltpu.VMEM((2, PEER_TOPK, tm), F32), pltpu.VMEM((2, PEER_TOPK, tm), F32),
                        pltpu.VMEM((PEER_TOPK, tm), F32), pltpu.VMEM((PEER_TOPK, tm), F32),
                        pltpu.VMEM((PEER_PAIRS, tm), F32), pltpu.VMEM((PEER_PAIRS, tm), F32)],
        compiler_params=pltpu.CompilerParams(
            dimension_semantics=("parallel", "arbitrary"), vmem_limit_bytes=VMEM_LIMIT_DENSE),
        name="peer_topk",
    )(qp, keys, pos)


WORD_ROWS = TILE_ROWS // 2
SLOTS = PEER_PAIRS // 2
SLOT_ROWS = SLOTS * TILE_ROWS
MXU_ROWS = 256
SLOTS_PER_MXU_TILE = MXU_ROWS // TILE_ROWS
PEER_TOKEN_BLOCK = 32


def _gather_slots(tile_ref, tab_ref, t, scale=None):
    def tile(j):
        start = pl.multiple_of(tile_ref[t, j], WORD_ROWS)
        x = pltpu.bitcast(tab_ref[pl.ds(start, WORD_ROWS), :], BF16)
        return x if scale is None else x * scale

    blocks = []
    for k in range(SLOTS // SLOTS_PER_MXU_TILE):
        slots = [jnp.concatenate([tile(2 * i), tile(2 * i + 1)], axis=1)
                 for i in range(k * SLOTS_PER_MXU_TILE, (k + 1) * SLOTS_PER_MXU_TILE)]
        blocks.append(jnp.concatenate(slots, axis=0))
    return blocks


def _peer_u_kernel(tile_ref, h_ref, par_ref, gate_ref, half_ref, tab_ref, act_ref, za_ref, zb_ref):
    tb = h_ref.shape[0]
    ones = jnp.ones((BF16_SUBLANES, LANES), BF16)
    half_sum = half_ref[...]

    def gather(t):
        ht = h_ref[t].astype(BF16)
        return _gather_slots(tile_ref, tab_ref, t, scale=jnp.concatenate([ht, ht], axis=0))

    def half_sums(prod):
        z = [_dot(half_sum, block) for block in prod]
        return jnp.concatenate(z, axis=0).astype(BF16)

    def lane_sums(t, z):
        s = _dot_nt(ones, jnp.concatenate([z[:, :LANES], z[:, LANES:]], axis=0))
        za_ref[t:t + 1, :] = s[0:1, :PEER_PAIRS]
        zb_ref[t:t + 1, :] = s[0:1, PEER_PAIRS:]

    prod = gather(0)
    pending = None
    for t in range(tb):
        nxt = gather(t + 1) if t + 1 < tb else None
        z = half_sums(prod)
        if pending is not None:
            lane_sums(t - 1, pending)
        pending, prod = z, nxt
    lane_sums(tb - 1, pending)
    za = za_ref[...]
    zb = zb_ref[...]
    first = par_ref[...] == 0.0
    even_pair = jnp.where(first, za, pltpu.roll(za, LANES - 1, 1))
    odd_pair = jnp.where(first, pltpu.roll(zb, 1, 1), zb)
    lane = lax.broadcasted_iota(jnp.int32, za.shape, 1)
    a = jnp.where(lane % 2 == 0, even_pair, odd_pair)
    act_ref[...] = 0.5 * a * (1.0 + lax.erf(a * (2.0 ** -0.5))) * gate_ref[...]


def _peer_u(tile, h3, par, gate, tab, tb):
    t = tile.shape[0]
    r = jnp.arange(2 * SLOTS_PER_MXU_TILE)[:, None]
    c = jnp.arange(MXU_ROWS)[None, :]
    half_sum = ((c // TILE_ROWS == r // 2) & ((c % TILE_ROWS) // ROWS_PER_EXPERT == r % 2)).astype(BF16)
    row = pl.BlockSpec((tb, PEER_PAIRS), lambda i: (i, 0))
    return pl.pallas_call(
        _peer_u_kernel,
        grid=(t // tb,),
        in_specs=[
            pl.BlockSpec((tb, PEER_PAIRS), lambda i: (i, 0), memory_space=pltpu.SMEM),
            pl.BlockSpec((tb, ROWS_PER_EXPERT, LANES), lambda i: (i, 0, 0)),
            row, row,
            pl.BlockSpec(half_sum.shape, lambda i: (0, 0)),
            pl.BlockSpec(memory_space=pltpu.VMEM),
        ],
        out_specs=row,
        out_shape=jax.ShapeDtypeStruct((t, PEER_PAIRS), F32),
        scratch_shapes=[pltpu.VMEM((tb, PEER_PAIRS), F32), pltpu.VMEM((tb, PEER_PAIRS), F32)],
        compiler_params=pltpu.CompilerParams(
            dimension_semantics=("parallel",), vmem_limit_bytes=VMEM_LIMIT_TABLE),
        name="peer_u",
    )(tile, h3, par, gate, half_sum, tab)


def _peer_v_kernel(tile_ref, act_ref, par_ref, exp_ref, tab_ref, y_ref, coef_ref):
    tb = act_ref.shape[0]
    n = ROWS_PER_EXPERT
    act = act_ref[...]
    odd = par_ref[...]
    parts = []
    for a in (act * (1.0 - odd), act * odd):
        hi = a.astype(BF16)
        parts.append((hi, (a - hi.astype(F32)).astype(BF16)))
    lhs = jnp.concatenate([jnp.concatenate([parts[0][k], parts[1][k]], axis=1) for k in range(2)], axis=0)
    lhs = jnp.broadcast_to(lhs[:, None, :], (2 * tb, n, 2 * PEER_PAIRS)).reshape(2 * tb * n, 2 * PEER_PAIRS)
    coef_ref[...] = _dot(lhs, exp_ref[...])
    shape = (n, SLOT_ROWS)
    own_row = (lax.broadcasted_iota(jnp.int32, shape, 1) % n) == lax.broadcasted_iota(jnp.int32, shape, 0)

    def consume(t, xs):
        def part(kind, side):
            r0 = (kind * tb + t) * n
            return jnp.where(own_row, coef_ref[r0:r0 + n, side * SLOT_ROWS:(side + 1) * SLOT_ROWS], 0.0)

        coef = jnp.concatenate([part(0, 0), part(0, 1), part(1, 0), part(1, 1)], axis=0).astype(BF16)
        res = _dot(coef, jnp.concatenate(xs, axis=0))
        y_ref[t] =((res[0:n, :LANES] + res[2 * n:3 * n, :LANES])
                    + (res[n:2 * n, LANES:] + res[3 * n:4 * n, LANES:]))

    xs = _gather_slots(tile_ref, tab_ref, 0)
    for t in range(tb):
        nxt = _gather_slots(tile_ref, tab_ref, t + 1) if t + 1 < tb else None
        consume(t, xs)
        xs = nxt


def _peer_v(tile, act, par, tab, tb):
    t = tile.shape[0]
    k = jnp.arange(2 * PEER_PAIRS)[:, None]
    c = jnp.arange(2 * SLOT_ROWS)[None, :]
    pair_of_col = 2 * ((c % SLOT_ROWS) // TILE_ROWS) + c // SLOT_ROWS
    half_of_col = (c % TILE_ROWS) // ROWS_PER_EXPERT
    expand = ((pair_of_col == k % PEER_PAIRS) & (half_of_col == k // PEER_PAIRS)).astype(BF16)
    row = pl.BlockSpec((tb, PEER_PAIRS), lambda i: (i, 0))
    return pl.pallas_call(
        _peer_v_kernel,
        grid=(t // tb,),
        in_specs=[
            pl.BlockSpec((tb, PEER_PAIRS), lambda i: (i, 0), memory_space=pltpu.SMEM),
            row, row,
            pl.BlockSpec(expand.shape, lambda i: (0, 0)),
            pl.BlockSpec(memory_space=pltpu.VMEM),
        ],
        out_specs=pl.BlockSpec((tb, ROWS_PER_EXPERT, LANES), lambda i: (i, 0, 0)),
        out_shape=jax.ShapeDtypeStruct((t, ROWS_PER_EXPERT, LANES), F32),
        scratch_shapes=[pltpu.VMEM((2 * tb * ROWS_PER_EXPERT, 2 * SLOT_ROWS), F32)],
        compiler_params=pltpu.CompilerParams(
            dimension_semantics=("parallel",), vmem_limit_bytes=VMEM_LIMIT_TABLE),
        name="peer_v",
    )(tile, act, par, expand, tab)


def _ple_kernel(x1_ref, y_ref, p_ref, g_ref, wg_ref, wp_ref, o_ref):
    y = jnp.concatenate([y_ref[:, r, :] for r in range(ROWS_PER_EXPERT)], axis=1)
    x2 = x1_ref[...] + y
    hg = _rms(x2, g_ref[...])
    gate = jax.nn.sigmoid(_dot(hg.astype(BF16), wg_ref[...]))
    o_ref[...] = x2 + gate * _dot(p_ref[...].astype(BF16), wp_ref[...])


def _ple(x1, y, p2d, g, wg, wp, tm):
    t, d = x1.shape
    row = lambda w: pl.BlockSpec((tm, w), lambda i: (i, 0))
    full = lambda a: pl.BlockSpec(a.shape, lambda i: (0,) * a.ndim)
    return pl.pallas_call(
        _ple_kernel,
        grid=(t // tm,),
        in_specs=[row(d), pl.BlockSpec((tm, ROWS_PER_EXPERT, LANES), lambda i: (i, 0, 0)), row(p2d.shape[1]),
                  full(g), full(wg), full(wp)],
        out_specs=row(d),
        out_shape=jax.ShapeDtypeStruct((t, d), F32),
        compiler_params=pltpu.CompilerParams(
            dimension_semantics=("parallel",), vmem_limit_bytes=VMEM_LIMIT_DENSE),
        name="ple",
    )(x1, y, p2d, g, wg, wp)


def _expert_table(emb):
    n, d = emb.shape
    assert d == ROWS_PER_EXPERT * LANES and n % 2 == 0
    tiles = emb.astype(BF16).reshape(n // 2, WORD_ROWS, 2, LANES)
    words = lax.bitcast_convert_type(jnp.swapaxes(tiles, -1, -2), jnp.uint32)
    return words.reshape(n // 2 * WORD_ROWS, LANES)


def _layer(x, p, g_mix, w_in, q_norm, k_norm, sinks, w_out, g_ffn, w_query, sub_keys, emb_u, emb_v,
           g_ple, w_gate, w_proj):
    b, s, d = x.shape
    t = b * s
    tm = 256 if t % 256 == 0 else CHUNK
    tb = PEER_TOKEN_BLOCK
    assert t % tb == 0
    x2d = x.reshape(t, d)
    proj = _in_proj(x2d, g_mix[None, :], w_in.astype(BF16), tm)
    proj3 = proj.reshape(b, s, proj.shape[1])
    ro = _retention(proj3, _retention_tables(s))
    so = _swa(proj3, sinks, q_norm[None, :], k_norm[None, :])
    wo = w_out.astype(BF16)
    x1, h3, qp = _out_proj(x2d, ro.reshape(t, RET_WIDTH), so.reshape(t, SWA_Q_WIDTH),
                           wo[:RET_WIDTH], wo[RET_WIDTH:], g_ffn[None, :], w_query.astype(BF16), tm)
    tile, par, gate = _peer_topk(qp, sub_keys.astype(BF16), tm)
    act = _peer_u(tile, h3, par, gate, _expert_table(emb_u), tb)
    y = _peer_v(tile, act, par, _expert_table(emb_v), tb)
    out = _ple(x1, y, p.reshape(t, p.shape[-1]), g_ple[None, :],
               w_gate.astype(BF16), w_proj.astype(BF16), tm)
    return out.reshape(b, s, d)


def kernel(x, p, g_mix, w_in, q_norm, k_norm, sinks, w_out, g_ffn, peer_w_query, peer_sub_keys,
           peer_u, peer_v, g_ple, w_ple_gate, w_ple_proj):
    for i in range(p.shape[0]):
        x = _layer(x, p[i], g_mix[i], w_in[i], q_norm[i], k_norm[i], sinks[i], w_out[i], g_ffn[i],
                   peer_w_query[i], peer_sub_keys[i], peer_u[i], peer_v[i], g_ple[i], w_ple_gate[i],
                   w_ple_proj[i])
    return x
```

```python
import functools

import jax
import jax.numpy as jnp
from jax import lax
from jax.experimental import pallas as pl
from jax.experimental.pallas import tpu as pltpu

F32 = jnp.float32
BF16 = jnp.bfloat16

HEAD_DIM = 64
RET_HEADS = 8
SWA_Q_HEADS = 8
SWA_KV_HEADS = 2
SWA_GROUP = SWA_Q_HEADS // SWA_KV_HEADS
RET_WIDTH = RET_HEADS * HEAD_DIM
SWA_Q_WIDTH = SWA_Q_HEADS * HEAD_DIM
SWA_KV_WIDTH = SWA_KV_HEADS * HEAD_DIM
CHUNK = 128
ROPE_BASE = 10000.0
PEER_HEADS = 8
PEER_KEYS = 128
PEER_HALF = 128
PEER_TOPK = 16
PEER_PAIRS = PEER_HEADS * PEER_TOPK
EPS = 1e-6

LANES = 128
F32_SUBLANES = 8
BF16_SUBLANES = 16
VMEM_LIMIT_DENSE = 48 * 1024 * 1024
VMEM_LIMIT_TABLE = 56 * 1024 * 1024

ROWS_PER_EXPERT = 1024 // LANES
TILE_ROWS = BF16_SUBLANES


def _rms(x, g):
    ms = jnp.mean(x * x, axis=-1, keepdims=True)
    return x * lax.rsqrt(ms + EPS) * g


def _dot(a, b):
    return jnp.dot(a, b, preferred_element_type=F32)


def _dot_nt(a, b):
    return lax.dot_general(a, b, (((1,), (1,)), ((), ())), preferred_element_type=F32)


def _dot_tn(a, b):
    return lax.dot_general(a, b, (((0,), (0,)), ((), ())), preferred_element_type=F32)


def _in_proj_kernel(x_ref, g_ref, w_ref, o_ref):
    h = _rms(x_ref[...], g_ref[...])
    o_ref[...] = _dot(h.astype(BF16), w_ref[...])


def _in_proj(x2d, g, w, tm):
    t, d = x2d.shape
    n = w.shape[1]
    return pl.pallas_call(
        _in_proj_kernel,
        grid=(t // tm,),
        in_specs=[
            pl.BlockSpec((tm, d), lambda i: (i, 0)),
            pl.BlockSpec((1, d), lambda i: (0, 0)),
            pl.BlockSpec((d, n), lambda i: (0, 0)),
        ],
        out_specs=pl.BlockSpec((tm, n), lambda i: (i, 0)),
        out_shape=jax.ShapeDtypeStruct((t, n), F32),
        compiler_params=pltpu.CompilerParams(
            dimension_semantics=("parallel",), vmem_limit_bytes=VMEM_LIMIT_DENSE),
        name="in_proj",
    )(x2d, g, w)


def _retention_kernel(cd_ref, q_ref, k_ref, v_ref, g_ref, cos_ref, sin_ref, dec_ref, wk_ref, wq_ref,
                      o_ref, state_ref):
    n = pl.program_id(1)

    @pl.when(n == 0)
    def _():
        state_ref[...] = jnp.zeros_like(state_ref)

    c = cos_ref[...]
    s = sin_ref[...]
    lane = lax.broadcasted_iota(jnp.int32, (CHUNK, RET_WIDTH), 1)
    first_half = (lane % HEAD_DIM) < (HEAD_DIM // 2)

    def rot(x):
        partner = jnp.where(first_half,
                            pltpu.roll(x, RET_WIDTH - HEAD_DIM // 2, 1),
                            pltpu.roll(x, HEAD_DIM // 2, 1))
        return x * c + partner * s

    q = rot(q_ref[...])
    k = rot(k_ref[...]) * (HEAD_DIM ** -0.5)
    kw = k * wk_ref[...]
    v = v_ref[...]
    gate = g_ref[...]
    wq = wq_ref[...]
    for h in range(RET_HEADS):
        sl = slice(h * HEAD_DIM, (h + 1) * HEAD_DIM)
        qh = q[:, sl].astype(BF16)
        kh = k[:, sl].astype(BF16)
        vh = v[:, sl].astype(BF16)
        att = _dot_nt(qh, kh) * dec_ref[h]
        o = _dot(att.astype(BF16), vh)
        st = state_ref[h]
        o = o + _dot(qh, st.astype(BF16)) * wq[:, sl]
        state_ref[h] = st * cd_ref[h] + _dot_tn(kw[:, sl].astype(BF16), vh)
        mu = jnp.mean(o, axis=-1, keepdims=True)
        oc = o - mu
        var = jnp.mean(oc * oc, axis=-1, keepdims=True)
        on = oc * lax.rsqrt(var + EPS)
        gh = gate[:, sl]
        o_ref[:, sl] = gh * jax.nn.sigmoid(gh) * on


def _retention(proj3, tables):
    b, s, _ = proj3.shape
    cd, cos_t, sin_t, dec, wk, wq = tables
    col = lambda j: pl.BlockSpec((None, CHUNK, RET_WIDTH), lambda bi, ni, j=j: (bi, ni, j))
    pos_spec = pl.BlockSpec((CHUNK, RET_WIDTH), lambda bi, ni: (ni, 0))
    const2 = pl.BlockSpec((CHUNK, RET_WIDTH), lambda bi, ni: (0, 0))
    return pl.pallas_call(
        _retention_kernel,
        grid=(b, s // CHUNK),
        in_specs=[
            pl.BlockSpec(memory_space=pltpu.SMEM),
            col(0), col(1), col(2), col(3),
            pos_spec, pos_spec,
            pl.BlockSpec((RET_HEADS, CHUNK, CHUNK), lambda bi, ni: (0, 0, 0)),
            const2, const2,
        ],
        out_specs=pl.BlockSpec((None, CHUNK, RET_WIDTH), lambda bi, ni: (bi, ni, 0)),
        out_shape=jax.ShapeDtypeStruct((b, s, RET_WIDTH), F32),
        scratch_shapes=[pltpu.VMEM((RET_HEADS, HEAD_DIM, HEAD_DIM), F32)],
        compiler_params=pltpu.CompilerParams(
            dimension_semantics=("parallel", "arbitrary"), vmem_limit_bytes=VMEM_LIMIT_DENSE),
        name="retention",
    )(cd, proj3, proj3, proj3, proj3, cos_t, sin_t, dec, wk, wq)


def _retention_tables(s):
    half = HEAD_DIM // 2
    pos = jnp.arange(s, dtype=F32)
    freqs = ROPE_BASE ** (-jnp.arange(half, dtype=F32) / half)
    ang = pos[:, None] * freqs[None, :]
    c = jnp.cos(ang)
    sn = jnp.sin(ang)
    cos_t = jnp.tile(jnp.concatenate([c, c], axis=-1), (1, RET_HEADS))
    sin_t = jnp.tile(jnp.concatenate([-sn, sn], axis=-1), (1, RET_HEADS))
    gamma = 1.0 - jnp.exp2(-5.0 - jnp.arange(RET_HEADS, dtype=F32))
    log_g = jnp.log(gamma)
    idx = jnp.arange(CHUNK)
    rel = idx[:, None] - idx[None, :]
    dec = jnp.where(rel[None] >= 0,
                    jnp.exp(log_g[:, None, None] * jnp.maximum(rel, 0)[None].astype(F32)),
                    0.0)
    w_k = jnp.exp(log_g[None, :] * (CHUNK - 1 - idx)[:, None].astype(F32))
    w_q = jnp.exp(log_g[None, :] * (idx + 1)[:, None].astype(F32))
    cd = jnp.exp(log_g * CHUNK)
    wk = jnp.repeat(w_k, HEAD_DIM, axis=1)
    wq = jnp.repeat(w_q, HEAD_DIM, axis=1)
    return cd, cos_t, sin_t, dec, wk, wq


def _swa_kernel(sink_ref, q_ref, kc_ref, vc_ref, kp_ref, vp_ref, qn_ref, kn_ref, o_ref):
    n = pl.program_id(1)
    qi = lax.broadcasted_iota(jnp.int32, (CHUNK, 2 * CHUNK), 0)
    ki = lax.broadcasted_iota(jnp.int32, (CHUNK, 2 * CHUNK), 1)
    rel = qi + CHUNK - ki
    mask = (rel >= 0) & (rel < CHUNK) & ((n > 0) | (ki >= CHUNK))
    q = q_ref[...]
    qn = qn_ref[...]
    kn = kn_ref[...]
    for g in range(SWA_KV_HEADS):
        sl = slice(g * HEAD_DIM, (g + 1) * HEAD_DIM)
        kcat = jnp.concatenate([kp_ref[:, sl], kc_ref[:, sl]], axis=0)
        kcat = _rms(kcat, kn).astype(BF16)
        vcat = jnp.concatenate([vp_ref[:, sl], vc_ref[:, sl]], axis=0).astype(BF16)
        for j in range(SWA_GROUP):
            h = g * SWA_GROUP + j
            hs = slice(h * HEAD_DIM, (h + 1) * HEAD_DIM)
            qh = _rms(q[:, hs], qn).astype(BF16)
            sc = _dot_nt(qh, kcat) * (HEAD_DIM ** -0.5)
            sc = jnp.where(mask, sc, -jnp.inf)
            sink = sink_ref[h]
            m = jnp.maximum(jnp.max(sc, axis=-1, keepdims=True), sink)
            e = jnp.exp(sc - m)
            pr = e / (jnp.sum(e, axis=-1, keepdims=True) + jnp.exp(sink - m))
            o_ref[:, hs] = _dot(pr.astype(BF16), vcat)


def _swa(proj3, sinks, qn, kn):
    b, s, _ = proj3.shape
    q_blk = (4 * RET_WIDTH) // SWA_Q_WIDTH
    k_blk = (4 * RET_WIDTH + SWA_Q_WIDTH) // SWA_KV_WIDTH
    v_blk = k_blk + 1
    cur = lambda j: pl.BlockSpec((None, CHUNK, SWA_KV_WIDTH), lambda bi, ni, j=j: (bi, ni, j))
    prev = lambda j: pl.BlockSpec((None, CHUNK, SWA_KV_WIDTH),
                                  lambda bi, ni, j=j: (bi, jnp.maximum(ni - 1, 0), j))
    return pl.pallas_call(
        _swa_kernel,
        grid=(b, s // CHUNK),
        in_specs=[
            pl.BlockSpec(memory_space=pltpu.SMEM),
            pl.BlockSpec((None, CHUNK, SWA_Q_WIDTH), lambda bi, ni: (bi, ni, q_blk)),
            cur(k_blk), cur(v_blk), prev(k_blk), prev(v_blk),
            pl.BlockSpec((1, HEAD_DIM), lambda bi, ni: (0, 0)),
            pl.BlockSpec((1, HEAD_DIM), lambda bi, ni: (0, 0)),
        ],
        out_specs=pl.BlockSpec((None, CHUNK, SWA_Q_WIDTH), lambda bi, ni: (bi, ni, 0)),
        out_shape=jax.ShapeDtypeStruct((b, s, SWA_Q_WIDTH), F32),
        compiler_params=pltpu.CompilerParams(
            dimension_semantics=("parallel", "parallel"), vmem_limit_bytes=VMEM_LIMIT_DENSE),
        name="swa",
    )(sinks, proj3, proj3, proj3, proj3, proj3, qn, kn)


def _out_proj_kernel(x_ref, ro_ref, so_ref, wo1_ref, wo2_ref, g_ref, wq_ref, x1_ref, h2_ref, qp_ref):
    x1 = (x_ref[...] + _dot(ro_ref[...].astype(BF16), wo1_ref[...])
          + _dot(so_ref[...].astype(BF16), wo2_ref[...]))
    x1_ref[...] = x1
    h2 = _rms(x1, g_ref[...])
    for r in range(ROWS_PER_EXPERT):
        h2_ref[:, r, :] = h2[:, r * LANES:(r + 1) * LANES]
    qp = _dot(h2.astype(BF16), wq_ref[...])
    width = qp_ref.shape[2]
    for h in range(PEER_HEADS):
        qp_ref[h] = qp[:, h * width:(h + 1) * width]


def _out_proj(x2d, ro, so, wo1, wo2, g, wq, tm):
    t, d = x2d.shape
    nq = wq.shape[1]
    width = nq // PEER_HEADS
    row = lambda w: pl.BlockSpec((tm, w), lambda i: (i, 0))
    full = lambda a: pl.BlockSpec(a.shape, lambda i: (0,) * a.ndim)
    return pl.pallas_call(
        _out_proj_kernel,
        grid=(t // tm,),
        in_specs=[row(d), row(ro.shape[1]), row(so.shape[1]), full(wo1), full(wo2), full(g), full(wq)],
        out_specs=[row(d), pl.BlockSpec((tm, ROWS_PER_EXPERT, LANES), lambda i: (i, 0, 0)),
                   pl.BlockSpec((PEER_HEADS, tm, width), lambda i: (0, i, 0))],
        out_shape=[jax.ShapeDtypeStruct((t, d), F32), jax.ShapeDtypeStruct((t, ROWS_PER_EXPERT, LANES), F32),
                   jax.ShapeDtypeStruct((PEER_HEADS, t, width), F32)],
        compiler_params=pltpu.CompilerParams(
            dimension_semantics=("parallel",), vmem_limit_bytes=VMEM_LIMIT_DENSE),
        name="out_proj",
    )(x2d, ro, so, wo1, wo2, g, wq)


_CAND_B = (16, 8, 8, 8, 8, 8, 8, 8)


def _candidate_flat_index():
    pos = [a * PEER_TOPK + b for a, nb in enumerate(_CAND_B) for b in range(nb)]
    pos += [a * PEER_TOPK for a in range(len(_CAND_B), PEER_TOPK)]
    return pos


def _peer_topk_kernel(qp_ref, keys_ref, pos_ref, idx_ref, gate_ref,
                      s_ref, i_ref, best_ref, exp_ref, et_ref, gt_ref):
    h = pl.program_id(1)
    tm = qp_ref.shape[0]
    neg = -jnp.inf
    key_id = lax.broadcasted_iota(jnp.int32, (PEER_KEYS, tm), 0).astype(F32)
    for p in range(2):
        qhp = qp_ref[:, p * PEER_HALF:(p + 1) * PEER_HALF].astype(BF16)
        sc = _dot_nt(keys_ref[p], qhp)
        for a in range(PEER_TOPK):
            m = jnp.max(sc, axis=0, keepdims=True)
            i = jnp.min(jnp.where(sc == m, key_id, float(PEER_KEYS)), axis=0, keepdims=True)
            sc = jnp.where(key_id == i, neg, sc)
            s_ref[p, a:a + 1, :] = m
            i_ref[p, a:a + 1, :] = i
    s0, s1 = s_ref[0], s_ref[1]
    i0, i1 = i_ref[0] * float(PEER_KEYS), i_ref[1]
    cs_blocks, ci_blocks = [], []
    for a, nb in enumerate(_CAND_B):
        cs_blocks.append(s0[a:a + 1] + s1[0:nb])
        ci_blocks.append(i0[a:a + 1] + i1[0:nb])
    na = len(_CAND_B)
    cs_blocks.append(s0[na:] + s1[0:1])
    ci_blocks.append(i0[na:] + i1[0:1])
    cs = jnp.concatenate(cs_blocks, axis=0)
    ci = jnp.concatenate(ci_blocks, axis=0)
    pos = pos_ref[...]
    for k in range(PEER_TOPK):
        m = jnp.max(cs, axis=0, keepdims=True)
        first = jnp.min(jnp.where(cs == m, pos, float(PEER_TOPK * PEER_TOPK)), axis=0, keepdims=True)
        sel = pos == first
        best_ref[k:k + 1, :] = m
        exp_ref[k:k + 1, :] = jnp.max(jnp.where(sel, ci, -1.0), axis=0, keepdims=True)
        cs = jnp.where(sel, neg, cs)
    best = best_ref[...]
    ex = jnp.exp(best - best[0:1])
    gate = ex / jnp.sum(ex, axis=0, keepdims=True)
    rows = pl.ds(pl.multiple_of(h * PEER_TOPK, PEER_TOPK), PEER_TOPK)
    et_ref[rows, :] = exp_ref[...]
    gt_ref[rows, :] = gate

    @pl.when(h == pl.num_programs(1) - 1)
    def _():
        idx_ref[...] = et_ref[...].T.astype(jnp.int32) * EXPERT_WORDS
        gate_ref[...] = gt_ref[...].T


def _peer_topk(qp, keys, tm):
    nh, t, width = qp.shape
    pos = _candidate_flat_index()
    pos = jnp.broadcast_to(jnp.asarray(pos, F32)[:, None], (len(pos), tm))
    full = lambda a: pl.BlockSpec(a.shape, lambda i, h: (0,) * a.ndim)
    out_spec = pl.BlockSpec((tm, PEER_PAIRS), lambda i, h: (i, 0))
    return pl.pallas_call(
        _peer_topk_kernel,
        grid=(t // tm, nh),
        in_specs=[pl.BlockSpec((None, tm, width), lambda i, h: (h, i, 0)), full(keys), full(pos)],
        out_specs=[out_spec, out_spec],
        out_shape=[jax.ShapeDtypeStruct((t, PEER_PAIRS), jnp.int32),
                   jax.ShapeDtypeStruct((t, PEER_PAIRS), F32)],
        scratch_shapes=[pltpu.VMEM((2, PEER_TOPK, tm), F32), pltpu.VMEM((2, PEER_TOPK, tm), F32),
                        pltpu.VMEM((PEER_TOPK, tm), F32), pltpu.VMEM((PEER_TOPK, tm), F32),
                        pltpu.VMEM((PEER_PAIRS, tm), F32), pltpu.VMEM((PEER_PAIRS, tm), F32)],
        compiler_params=pltpu.CompilerParams(
            dimension_semantics=("parallel", "arbitrary"), vmem_limit_bytes=VMEM_LIMIT_DENSE),
        name="peer_topk",
    )(qp, keys, pos)


EXPERT_WORDS = ROWS_PER_EXPERT // 2
SLOTS = PEER_PAIRS // 4
SLOT_ROWS = SLOTS * TILE_ROWS
MXU_ROWS = 256
SLOTS_PER_MXU_TILE = MXU_ROWS // TILE_ROWS
INDEX_SLOT_TOKENS = 16
PEER_TOKEN_BLOCK = 2 * INDEX_SLOT_TOKENS


class _IndexSlots:
    def __init__(self, idx_hbm, idx_ref, sem_ref):
        self.idx_hbm, self.idx_ref, self.sem_ref = idx_hbm, idx_ref, sem_ref
        self.step = pl.program_id(0)
        self.last = pl.num_programs(0) - 1

    def _copy(self, step, slot):
        row0 = pl.multiple_of((2 * step + slot) * INDEX_SLOT_TOKENS, INDEX_SLOT_TOKENS)
        return pltpu.make_async_copy(self.idx_hbm.at[pl.ds(row0, INDEX_SLOT_TOKENS)],
                                     self.idx_ref.at[slot], self.sem_ref.at[slot])

    def prime(self):
        @pl.when(self.step == 0)
        def _():
            self._copy(self.step, 0).start()
            self._copy(self.step, 1).start()

    def before_gather(self, t):
        if t % INDEX_SLOT_TOKENS == 0:
            self._copy(self.step, t // INDEX_SLOT_TOKENS).wait()

    def after_gather(self, t):
        if t % INDEX_SLOT_TOKENS == INDEX_SLOT_TOKENS - 1:
            self._copy(jnp.minimum(self.step + 1, self.last), t // INDEX_SLOT_TOKENS).start()

    def drain(self):
        @pl.when(self.step == self.last)
        def _():
            self._copy(self.step, 0).wait()
            self._copy(self.step, 1).wait()

    def __call__(self, t, j):
        return self.idx_ref[t // INDEX_SLOT_TOKENS, t % INDEX_SLOT_TOKENS, j]


def _index_scratch():
    return [pltpu.SMEM((2, INDEX_SLOT_TOKENS, PEER_PAIRS), jnp.int32), pltpu.SemaphoreType.DMA((2,))]


def _gather_slots(index, tab_ref, t, scale=None):
    index.before_gather(t)

    def expert(j):
        row0 = pl.multiple_of(index(t, j), EXPERT_WORDS)
        return tab_ref[pl.ds(row0, EXPERT_WORDS), :]

    def tile(j):
        words = jnp.concatenate([expert(j), expert(j + 1)], axis=0)
        x = pltpu.bitcast(words, BF16)
        return x if scale is None else x * scale

    blocks = []
    for k in range(SLOTS // SLOTS_PER_MXU_TILE):
        slots = [jnp.concatenate([tile(2 * i), tile(PEER_PAIRS // 2 + 2 * i)], axis=1)
                 for i in range(k * SLOTS_PER_MXU_TILE, (k + 1) * SLOTS_PER_MXU_TILE)]
        blocks.append(jnp.concatenate(slots, axis=0))
    index.after_gather(t)
    return blocks


def _peer_u_kernel(idx_hbm, h_ref, gate_ref, half_ref, tab_ref, act_ref, idx_ref, sem_ref):
    tb = h_ref.shape[0]
    ones = jnp.ones((BF16_SUBLANES, LANES), BF16)
    half_sum = half_ref[...]
    index = _IndexSlots(idx_hbm, idx_ref, sem_ref)
    index.prime()

    def gather(t):
        ht = h_ref[t].astype(BF16)
        return _gather_slots(index, tab_ref, t, scale=jnp.concatenate([ht, ht], axis=0))

    def half_sums(prod):
        z = [_dot(half_sum, block) for block in prod]
        return jnp.concatenate(z, axis=0).astype(BF16)

    def lane_sums(t, z):
        s = _dot_nt(ones, jnp.concatenate([z[:, :LANES], z[:, LANES:]], axis=0))
        act_ref[t:t + 1, :] = s[0:1, :]

    prod = gather(0)
    pending = None
    for t in range(tb):
        nxt = gather(t + 1) if t + 1 < tb else None
        z = half_sums(prod)
        if pending is not None:
            lane_sums(t - 1, pending)
        pending, prod = z, nxt
    lane_sums(tb - 1, pending)
    index.drain()
    a = act_ref[...]
    act_ref[...] = 0.5 * a * (1.0 + lax.erf(a * (2.0 ** -0.5))) * gate_ref[...]


def _peer_u(idx, h3, gate, tab, tb):
    t = idx.shape[0]
    r = jnp.arange(2 * SLOTS_PER_MXU_TILE)[:, None]
    c = jnp.arange(MXU_ROWS)[None, :]
    half_sum = ((c // TILE_ROWS == r // 2) & ((c % TILE_ROWS) // ROWS_PER_EXPERT == r % 2)).astype(BF16)
    row = pl.BlockSpec((tb, PEER_PAIRS), lambda i: (i, 0))
    return pl.pallas_call(
        _peer_u_kernel,
        grid=(t // tb,),
        in_specs=[
            pl.BlockSpec(memory_space=pl.ANY),
            pl.BlockSpec((tb, ROWS_PER_EXPERT, LANES), lambda i: (i, 0, 0)),
            row,
            pl.BlockSpec(half_sum.shape, lambda i: (0, 0)),
            pl.BlockSpec(memory_space=pltpu.VMEM),
        ],
        out_specs=row,
        out_shape=jax.ShapeDtypeStruct((t, PEER_PAIRS), F32),
        scratch_shapes=_index_scratch(),
        compiler_params=pltpu.CompilerParams(
            dimension_semantics=("arbitrary",), vmem_limit_bytes=VMEM_LIMIT_TABLE),
        name="peer_u",
    )(idx, h3, gate, half_sum, tab)


def _peer_v_kernel(idx_hbm, act_ref, exp_ref, tab_ref, y_ref, coef_ref, idx_ref, sem_ref):
    tb = act_ref.shape[0]
    n = ROWS_PER_EXPERT
    index = _IndexSlots(idx_hbm, idx_ref, sem_ref)
    index.prime()
    act = act_ref[...]
    hi = act.astype(BF16)
    lhs = jnp.concatenate([hi, (act - hi.astype(F32)).astype(BF16)], axis=0)
    lhs = jnp.broadcast_to(lhs[:, None, :], (2 * tb, n, PEER_PAIRS)).reshape(2 * tb * n, PEER_PAIRS)
    coef_ref[...] = _dot(lhs, exp_ref[...])
    shape = (n, SLOT_ROWS)
    own_row = (lax.broadcasted_iota(jnp.int32, shape, 1) % n) == lax.broadcasted_iota(jnp.int32, shape, 0)

    def consume(t, xs):
        def part(kind, side):
            r0 = (kind * tb + t) * n
            return jnp.where(own_row, coef_ref[r0:r0 + n, side * SLOT_ROWS:(side + 1) * SLOT_ROWS], 0.0)

        coef = jnp.concatenate([part(0, 0), part(0, 1), part(1, 0), part(1, 1)], axis=0).astype(BF16)
        res = _dot(coef, jnp.concatenate(xs, axis=0))
        y_ref[t] =((res[0:n, :LANES] + res[2 * n:3 * n, :LANES])
                    + (res[n:2 * n, LANES:] + res[3 * n:4 * n, LANES:]))

    xs = _gather_slots(index, tab_ref, 0)
    for t in range(tb):
        nxt = _gather_slots(index, tab_ref, t + 1) if t + 1 < tb else None
        consume(t, xs)
        xs = nxt
    index.drain()


def _peer_v(idx, act, tab, tb):
    t = idx.shape[0]
    j = jnp.arange(PEER_PAIRS)[:, None]
    c = jnp.arange(2 * SLOT_ROWS)[None, :]
    pair_of_col = ((PEER_PAIRS // 2) * (c // SLOT_ROWS) + 2 * ((c % SLOT_ROWS) // TILE_ROWS)
                   + (c % TILE_ROWS) // ROWS_PER_EXPERT)
    expand = (pair_of_col == j).astype(BF16)
    row = pl.BlockSpec((tb, PEER_PAIRS), lambda i: (i, 0))
    return pl.pallas_call(
        _peer_v_kernel,
        grid=(t // tb,),
        in_specs=[
            pl.BlockSpec(memory_space=pl.ANY),
            row,
            pl.BlockSpec(expand.shape, lambda i: (0, 0)),
            pl.BlockSpec(memory_space=pltpu.VMEM),
        ],
        out_specs=pl.BlockSpec((tb, ROWS_PER_EXPERT, LANES), lambda i: (i, 0, 0)),
        out_shape=jax.ShapeDtypeStruct((t, ROWS_PER_EXPERT, LANES), F32),
        scratch_shapes=[pltpu.VMEM((2 * tb * ROWS_PER_EXPERT, 2 * SLOT_ROWS), F32)] + _index_scratch(),
        compiler_params=pltpu.CompilerParams(
            dimension_semantics=("arbitrary",), vmem_limit_bytes=VMEM_LIMIT_TABLE),
        name="peer_v",
    )(idx, act, expand, tab)


def _ple_kernel(x1_ref, y_ref, p_ref, g_ref, wg_ref, wp_ref, o_ref):
    y = jnp.concatenate([y_ref[:, r, :] for r in range(ROWS_PER_EXPERT)], axis=1)
    x2 = x1_ref[...] + y
    hg = _rms(x2, g_ref[...])
    gate = jax.nn.sigmoid(_dot(hg.astype(BF16), wg_ref[...]))
    o_ref[...] = x2 + gate * _dot(p_ref[...].astype(BF16), wp_ref[...])


def _ple(x1, y, p2d, g, wg, wp, tm):
    t, d = x1.shape
    row = lambda w: pl.BlockSpec((tm, w), lambda i: (i, 0))
    full = lambda a: pl.BlockSpec(a.shape, lambda i: (0,) * a.ndim)
    return pl.pallas_call(
        _ple_kernel,
        grid=(t // tm,),
        in_specs=[row(d), pl.BlockSpec((tm, ROWS_PER_EXPERT, LANES), lambda i: (i, 0, 0)), row(p2d.shape[1]),
                  full(g), full(wg), full(wp)],
        out_specs=row(d),
        out_shape=jax.ShapeDtypeStruct((t, d), F32),
        compiler_params=pltpu.CompilerParams(
            dimension_semantics=("parallel",), vmem_limit_bytes=VMEM_LIMIT_DENSE),
        name="ple",
    )(x1, y, p2d, g, wg, wp)


def _expert_table(emb):
    n, d = emb.shape
    assert d == ROWS_PER_EXPERT * LANES
    rows = emb.astype(BF16).reshape(n, EXPERT_WORDS, 2, LANES)
    words = lax.bitcast_convert_type(jnp.swapaxes(rows, -1, -2), jnp.uint32)
    return words.reshape(n * EXPERT_WORDS, LANES)


def _layer(x, p, g_mix, w_in, q_norm, k_norm, sinks, w_out, g_ffn, w_query, sub_keys, emb_u, emb_v,
           g_ple, w_gate, w_proj):
    b, s, d = x.shape
    t = b * s
    tm = 256 if t % 256 == 0 else CHUNK
    tb = PEER_TOKEN_BLOCK
    assert t % tb == 0
    x2d = x.reshape(t, d)
    proj = _in_proj(x2d, g_mix[None, :], w_in.astype(BF16), tm)
    proj3 = proj.reshape(b, s, proj.shape[1])
    ro = _retention(proj3, _retention_tables(s))
    so = _swa(proj3, sinks, q_norm[None, :], k_norm[None, :])
    wo = w_out.astype(BF16)
    x1, h3, qp = _out_proj(x2d, ro.reshape(t, RET_WIDTH), so.reshape(t, SWA_Q_WIDTH),
                           wo[:RET_WIDTH], wo[RET_WIDTH:], g_ffn[None, :], w_query.astype(BF16), tm)
    idx, gate = _peer_topk(qp, sub_keys.astype(BF16), tm)
    act = _peer_u(idx, h3, gate, _expert_table(emb_u), tb)
    y = _peer_v(idx, act, _expert_table(emb_v), tb)
    out = _ple(x1, y, p.reshape(t, p.shape[-1]), g_ple[None, :],
               w_gate.astype(BF16), w_proj.astype(BF16), tm)
    return out.reshape(b, s, d)


def kernel(x, p, g_mix, w_in, q_norm, k_norm, sinks, w_out, g_ffn, peer_w_query, peer_sub_keys,
           peer_u, peer_v, g_ple, w_ple_gate, w_ple_proj):
    for i in range(p.shape[0]):
        x = _layer(x, p[i], g_mix[i], w_in[i], q_norm[i], k_norm[i], sinks[i], w_out[i], g_ffn[i],
                   peer_w_query[i], peer_sub_keys[i], peer_u[i], peer_v[i], g_ple[i], w_ple_gate[i],
                   w_ple_proj[i])
    return x
```

```python
import functools

import jax
import jax.numpy as jnp
from jax import lax
from jax.experimental import pallas as pl
from jax.experimental.pallas import tpu as pltpu

F32 = jnp.float32
BF16 = jnp.bfloat16

HEAD_DIM = 64
RET_HEADS = 8
SWA_Q_HEADS = 8
SWA_KV_HEADS = 2
SWA_GROUP = SWA_Q_HEADS // SWA_KV_HEADS
RET_WIDTH = RET_HEADS * HEAD_DIM
SWA_Q_WIDTH = SWA_Q_HEADS * HEAD_DIM
SWA_KV_WIDTH = SWA_KV_HEADS * HEAD_DIM
CHUNK = 128
ROPE_BASE = 10000.0
PEER_HEADS = 8
PEER_KEYS = 128
PEER_HALF = 128
PEER_TOPK = 16
PEER_PAIRS = PEER_HEADS * PEER_TOPK
EPS = 1e-6

LANES = 128
F32_SUBLANES = 8
BF16_SUBLANES = 16
VMEM_LIMIT_DENSE = 48 * 1024 * 1024
VMEM_LIMIT_TABLE = 56 * 1024 * 1024

ROWS_PER_EXPERT = 1024 // LANES
TILE_ROWS = BF16_SUBLANES


def _rms(x, g):
    ms = jnp.mean(x * x, axis=-1, keepdims=True)
    return x * lax.rsqrt(ms + EPS) * g


def _dot(a, b):
    return jnp.dot(a, b, preferred_element_type=F32)


def _dot_nt(a, b):
    return lax.dot_general(a, b, (((1,), (1,)), ((), ())), preferred_element_type=F32)


def _dot_tn(a, b):
    return lax.dot_general(a, b, (((0,), (0,)), ((), ())), preferred_element_type=F32)


def _in_proj_kernel(x_ref, g_ref, w_ref, o_ref):
    h = _rms(x_ref[...], g_ref[...])
    o_ref[...] = _dot(h.astype(BF16), w_ref[...])


def _in_proj(x2d, g, w, tm):
    t, d = x2d.shape
    n = w.shape[1]
    return pl.pallas_call(
        _in_proj_kernel,
        grid=(t // tm,),
        in_specs=[
            pl.BlockSpec((tm, d), lambda i: (i, 0)),
            pl.BlockSpec((1, d), lambda i: (0, 0)),
            pl.BlockSpec((d, n), lambda i: (0, 0)),
        ],
        out_specs=pl.BlockSpec((tm, n), lambda i: (i, 0)),
        out_shape=jax.ShapeDtypeStruct((t, n), F32),
        compiler_params=pltpu.CompilerParams(
            dimension_semantics=("parallel",), vmem_limit_bytes=VMEM_LIMIT_DENSE),
        name="in_proj",
    )(x2d, g, w)


def _retention_kernel(cd_ref, q_ref, k_ref, v_ref, g_ref, cos_ref, sin_ref, dec_ref, wk_ref, wq_ref,
                      o_ref, state_ref):
    n = pl.program_id(1)

    @pl.when(n == 0)
    def _():
        state_ref[...] = jnp.zeros_like(state_ref)

    c = cos_ref[...]
    s = sin_ref[...]
    lane = lax.broadcasted_iota(jnp.int32, (CHUNK, RET_WIDTH), 1)
    first_half = (lane % HEAD_DIM) < (HEAD_DIM // 2)

    def rot(x):
        partner = jnp.where(first_half,
                            pltpu.roll(x, RET_WIDTH - HEAD_DIM // 2, 1),
                            pltpu.roll(x, HEAD_DIM // 2, 1))
        return x * c + partner * s

    wq = wq_ref[...]
    for bb in range(q_ref.shape[0]):
        q = rot(q_ref[bb])
        k = rot(k_ref[bb]) * (HEAD_DIM ** -0.5)
        kw = k * wk_ref[...]
        v = v_ref[bb]
        gate = g_ref[bb]
        for h in range(RET_HEADS):
            sl = slice(h * HEAD_DIM, (h + 1) * HEAD_DIM)
            qh = q[:, sl].astype(BF16)
            kh = k[:, sl].astype(BF16)
            vh = v[:, sl].astype(BF16)
            att = _dot_nt(qh, kh) * dec_ref[h]
            o = _dot(att.astype(BF16), vh)
            st = state_ref[bb, h]
            o = o + _dot(qh, st.astype(BF16)) * wq[:, sl]
            state_ref[bb, h] = st * cd_ref[h] + _dot_tn(kw[:, sl].astype(BF16), vh)
            mu = jnp.mean(o, axis=-1, keepdims=True)
            oc = o - mu
            var = jnp.mean(oc * oc, axis=-1, keepdims=True)
            on = oc * lax.rsqrt(var + EPS)
            gh = gate[:, sl]
            o_ref[bb, :, sl] = gh * jax.nn.sigmoid(gh) * on


def _retention(proj3, tables):
    b, s, _ = proj3.shape
    cd, cos_t, sin_t, dec, wk, wq = tables
    nb = 1
    col = lambda j: pl.BlockSpec((nb, CHUNK, RET_WIDTH), lambda bi, ni, j=j: (bi, ni, j))
    pos_spec = pl.BlockSpec((CHUNK, RET_WIDTH), lambda bi, ni: (ni, 0))
    const2 = pl.BlockSpec((CHUNK, RET_WIDTH), lambda bi, ni: (0, 0))
    return pl.pallas_call(
        _retention_kernel,
        grid=(b // nb, s // CHUNK),
        in_specs=[
            pl.BlockSpec(memory_space=pltpu.SMEM),
            col(0), col(1), col(2), col(3),
            pos_spec, pos_spec,
            pl.BlockSpec((RET_HEADS, CHUNK, CHUNK), lambda bi, ni: (0, 0, 0)),
            const2, const2,
        ],
        out_specs=pl.BlockSpec((nb, CHUNK, RET_WIDTH), lambda bi, ni: (bi, ni, 0)),
        out_shape=jax.ShapeDtypeStruct((b, s, RET_WIDTH), F32),
        scratch_shapes=[pltpu.VMEM((nb, RET_HEADS, HEAD_DIM, HEAD_DIM), F32)],
        compiler_params=pltpu.CompilerParams(
            dimension_semantics=("parallel", "arbitrary"), vmem_limit_bytes=VMEM_LIMIT_DENSE),
        name="retention",
    )(cd, proj3, proj3, proj3, proj3, cos_t, sin_t, dec, wk, wq)


def _retention_tables(s):
    half = HEAD_DIM // 2
    pos = jnp.arange(s, dtype=F32)
    freqs = ROPE_BASE ** (-jnp.arange(half, dtype=F32) / half)
    ang = pos[:, None] * freqs[None, :]
    c = jnp.cos(ang)
    sn = jnp.sin(ang)
    cos_t = jnp.tile(jnp.concatenate([c, c], axis=-1), (1, RET_HEADS))
    sin_t = jnp.tile(jnp.concatenate([-sn, sn], axis=-1), (1, RET_HEADS))
    gamma = 1.0 - jnp.exp2(-5.0 - jnp.arange(RET_HEADS, dtype=F32))
    log_g = jnp.log(gamma)
    idx = jnp.arange(CHUNK)
    rel = idx[:, None] - idx[None, :]
    dec = jnp.where(rel[None] >= 0,
                    jnp.exp(log_g[:, None, None] * jnp.maximum(rel, 0)[None].astype(F32)),
                    0.0)
    w_k = jnp.exp(log_g[None, :] * (CHUNK - 1 - idx)[:, None].astype(F32))
    w_q = jnp.exp(log_g[None, :] * (idx + 1)[:, None].astype(F32))
    cd = jnp.exp(log_g * CHUNK)
    wk = jnp.repeat(w_k, HEAD_DIM, axis=1)
    wq = jnp.repeat(w_q, HEAD_DIM, axis=1)
    return cd, cos_t, sin_t, dec, wk, wq


def _swa_kernel(sink_ref, q_ref, kc_ref, vc_ref, kp_ref, vp_ref, qn_ref, kn_ref, o_ref):
    n = pl.program_id(1)
    qi = lax.broadcasted_iota(jnp.int32, (CHUNK, 2 * CHUNK), 0)
    ki = lax.broadcasted_iota(jnp.int32, (CHUNK, 2 * CHUNK), 1)
    rel = qi + CHUNK - ki
    mask = (rel >= 0) & (rel < CHUNK) & ((n > 0) | (ki >= CHUNK))
    qn = qn_ref[...]
    kn = kn_ref[...]
    for bb in range(q_ref.shape[0]):
        q = q_ref[bb]
        for g in range(SWA_KV_HEADS):
            sl = slice(g * HEAD_DIM, (g + 1) * HEAD_DIM)
            kcat = jnp.concatenate([kp_ref[bb, :, sl], kc_ref[bb, :, sl]], axis=0)
            kcat = _rms(kcat, kn).astype(BF16)
            vcat = jnp.concatenate([vp_ref[bb, :, sl], vc_ref[bb, :, sl]], axis=0).astype(BF16)
            for j in range(SWA_GROUP):
                h = g * SWA_GROUP + j
                hs = slice(h * HEAD_DIM, (h + 1) * HEAD_DIM)
                qh = _rms(q[:, hs], qn).astype(BF16)
                sc = _dot_nt(qh, kcat) * (HEAD_DIM ** -0.5)
                sc = jnp.where(mask, sc, -jnp.inf)
                sink = sink_ref[h]
                m = jnp.maximum(jnp.max(sc, axis=-1, keepdims=True), sink)
                e = jnp.exp(sc - m)
                pr = e / (jnp.sum(e, axis=-1, keepdims=True) + jnp.exp(sink - m))
                o_ref[bb, :, hs] = _dot(pr.astype(BF16), vcat)


def _swa(proj3, sinks, qn, kn):
    b, s, _ = proj3.shape
    q_blk = (4 * RET_WIDTH) // SWA_Q_WIDTH
    k_blk = (4 * RET_WIDTH + SWA_Q_WIDTH) // SWA_KV_WIDTH
    v_blk = k_blk + 1
    nb = 1
    cur = lambda j: pl.BlockSpec((nb, CHUNK, SWA_KV_WIDTH), lambda bi, ni, j=j: (bi, ni, j))
    prev = lambda j: pl.BlockSpec((nb, CHUNK, SWA_KV_WIDTH),
                                  lambda bi, ni, j=j: (bi, jnp.maximum(ni - 1, 0), j))
    return pl.pallas_call(
        _swa_kernel,
        grid=(b // nb, s // CHUNK),
        in_specs=[
            pl.BlockSpec(memory_space=pltpu.SMEM),
            pl.BlockSpec((nb, CHUNK, SWA_Q_WIDTH), lambda bi, ni: (bi, ni, q_blk)),
            cur(k_blk), cur(v_blk), prev(k_blk), prev(v_blk),
            pl.BlockSpec((1, HEAD_DIM), lambda bi, ni: (0, 0)),
            pl.BlockSpec((1, HEAD_DIM), lambda bi, ni: (0, 0)),
        ],
        out_specs=pl.BlockSpec((nb, CHUNK, SWA_Q_WIDTH), lambda bi, ni: (bi, ni, 0)),
        out_shape=jax.ShapeDtypeStruct((b, s, SWA_Q_WIDTH), F32),
        compiler_params=pltpu.CompilerParams(
            dimension_semantics=("parallel", "parallel"), vmem_limit_bytes=VMEM_LIMIT_DENSE),
        name="swa",
    )(sinks, proj3, proj3, proj3, proj3, proj3, qn, kn)


def _out_proj_kernel(x_ref, ro_ref, so_ref, wo1_ref, wo2_ref, g_ref, wq_ref, x1_ref, h2_ref, qp_ref):
    x1 = (x_ref[...] + _dot(ro_ref[...].astype(BF16), wo1_ref[...])
          + _dot(so_ref[...].astype(BF16), wo2_ref[...]))
    x1_ref[...] = x1
    h2 = _rms(x1, g_ref[...])
    for r in range(ROWS_PER_EXPERT):
        h2_ref[:, r, :] = h2[:, r * LANES:(r + 1) * LANES]
    qp = _dot(h2.astype(BF16), wq_ref[...])
    width = qp_ref.shape[2]
    for h in range(PEER_HEADS):
        qp_ref[h] = qp[:, h * width:(h + 1) * width]


def _out_proj(x2d, ro, so, wo1, wo2, g, wq, tm):
    t, d = x2d.shape
    nq = wq.shape[1]
    width = nq // PEER_HEADS
    row = lambda w: pl.BlockSpec((tm, w), lambda i: (i, 0))
    full = lambda a: pl.BlockSpec(a.shape, lambda i: (0,) * a.ndim)
    return pl.pallas_call(
        _out_proj_kernel,
        grid=(t // tm,),
        in_specs=[row(d), row(ro.shape[1]), row(so.shape[1]), full(wo1), full(wo2), full(g), full(wq)],
        out_specs=[row(d), pl.BlockSpec((tm, ROWS_PER_EXPERT, LANES), lambda i: (i, 0, 0)),
                   pl.BlockSpec((PEER_HEADS, tm, width), lambda i: (0, i, 0))],
        out_shape=[jax.ShapeDtypeStruct((t, d), F32), jax.ShapeDtypeStruct((t, ROWS_PER_EXPERT, LANES), F32),
                   jax.ShapeDtypeStruct((PEER_HEADS, t, width), F32)],
        compiler_params=pltpu.CompilerParams(
            dimension_semantics=("parallel",), vmem_limit_bytes=VMEM_LIMIT_DENSE),
        name="out_proj",
    )(x2d, ro, so, wo1, wo2, g, wq)


_CAND_B = (16, 8, 8, 8, 8, 8, 8, 8)


def _candidate_flat_index():
    pos = [a * PEER_TOPK + b for a, nb in enumerate(_CAND_B) for b in range(nb)]
    pos += [a * PEER_TOPK for a in range(len(_CAND_B), PEER_TOPK)]
    return pos


def _peer_topk_kernel(qp_ref, keys_ref, pos_ref, idx_ref, gate_ref,
                      s_ref, i_ref, best_ref, exp_ref, et_ref, gt_ref):
    h = pl.program_id(1)
    tm = qp_ref.shape[0]
    neg = -jnp.inf
    key_id = lax.broadcasted_iota(jnp.int32, (PEER_KEYS, tm), 0).astype(F32)
    for p in range(2):
        qhp = qp_ref[:, p * PEER_HALF:(p + 1) * PEER_HALF].astype(BF16)
        sc = _dot_nt(keys_ref[p], qhp)
        for a in range(PEER_TOPK):
            m = jnp.max(sc, axis=0, keepdims=True)
            i = jnp.min(jnp.where(sc == m, key_id, float(PEER_KEYS)), axis=0, keepdims=True)
            sc = jnp.where(key_id == i, neg, sc)
            s_ref[p, a:a + 1, :] = m
            i_ref[p, a:a + 1, :] = i
    s0, s1 = s_ref[0], s_ref[1]
    i0, i1 = i_ref[0] * float(PEER_KEYS), i_ref[1]
    cs_blocks, ci_blocks = [], []
    for a, nb in enumerate(_CAND_B):
        cs_blocks.append(s0[a:a + 1] + s1[0:nb])
        ci_blocks.append(i0[a:a + 1] + i1[0:nb])
    na = len(_CAND_B)
    cs_blocks.append(s0[na:] + s1[0:1])
    ci_blocks.append(i0[na:] + i1[0:1])
    cs = jnp.concatenate(cs_blocks, axis=0)
    ci = jnp.concatenate(ci_blocks, axis=0)
    pos = pos_ref[...]
    for k in range(PEER_TOPK):
        m = jnp.max(cs, axis=0, keepdims=True)
        first = jnp.min(jnp.where(cs == m, pos, float(PEER_TOPK * PEER_TOPK)), axis=0, keepdims=True)
        sel = pos == first
        best_ref[k:k + 1, :] = m
        exp_ref[k:k + 1, :] = jnp.max(jnp.where(sel, ci, -1.0), axis=0, keepdims=True)
        cs = jnp.where(sel, neg, cs)
    best = best_ref[...]
    ex = jnp.exp(best - best[0:1])
    gate = ex / jnp.sum(ex, axis=0, keepdims=True)
    rows = pl.ds(pl.multiple_of(h * PEER_TOPK, PEER_TOPK), PEER_TOPK)
    et_ref[rows, :] = exp_ref[...]
    gt_ref[rows, :] = gate

    @pl.when(h == pl.num_programs(1) - 1)
    def _():
        idx_ref[...] = et_ref[...].T.astype(jnp.int32) * EXPERT_WORDS
        gate_ref[...] = gt_ref[...].T


def _peer_topk(qp, keys, tm):
    nh, t, width = qp.shape
    pos = _candidate_flat_index()
    pos = jnp.broadcast_to(jnp.asarray(pos, F32)[:, None], (len(pos), tm))
    full = lambda a: pl.BlockSpec(a.shape, lambda i, h: (0,) * a.ndim)
    out_spec = pl.BlockSpec((tm, PEER_PAIRS), lambda i, h: (i, 0))
    return pl.pallas_call(
        _peer_topk_kernel,
        grid=(t // tm, nh),
        in_specs=[pl.BlockSpec((None, tm, width), lambda i, h: (h, i, 0)), full(keys), full(pos)],
        out_specs=[out_spec, out_spec],
        out_shape=[jax.ShapeDtypeStruct((t, PEER_PAIRS), jnp.int32),
                   jax.ShapeDtypeStruct((t, PEER_PAIRS), F32)],
        scratch_shapes=[pltpu.VMEM((2, PEER_TOPK, tm), F32), pltpu.VMEM((2, PEER_TOPK, tm), F32),
                        pltpu.VMEM((PEER_TOPK, tm), F32), pltpu.VMEM((PEER_TOPK, tm), F32),
                        pltpu.VMEM((PEER_PAIRS, tm), F32), pltpu.VMEM((PEER_PAIRS, tm), F32)],
        compiler_params=pltpu.CompilerParams(
            dimension_semantics=("parallel", "arbitrary"), vmem_limit_bytes=VMEM_LIMIT_DENSE),
        name="peer_topk",
    )(qp, keys, pos)


EXPERT_WORDS = ROWS_PER_EXPERT // 2
SLOTS = PEER_PAIRS // 4
SLOT_ROWS = SLOTS * TILE_ROWS
MXU_ROWS = 256
SLOTS_PER_MXU_TILE = MXU_ROWS // TILE_ROWS
INDEX_SLOT_TOKENS = 32
PEER_TOKEN_BLOCK = 2 * INDEX_SLOT_TOKENS


class _IndexSlots:
    def __init__(self, idx_hbm, idx_ref, sem_ref):
        self.idx_hbm, self.idx_ref, self.sem_ref = idx_hbm, idx_ref, sem_ref
        self.step = pl.program_id(0)
        self.last = pl.num_programs(0) - 1

    def _copy(self, step, slot):
        row0 = pl.multiple_of((2 * step + slot) * INDEX_SLOT_TOKENS, INDEX_SLOT_TOKENS)
        return pltpu.make_async_copy(self.idx_hbm.at[pl.ds(row0, INDEX_SLOT_TOKENS)],
                                     self.idx_ref.at[slot], self.sem_ref.at[slot])

    def prime(self):
        @pl.when(self.step == 0)
        def _():
            self._copy(self.step, 0).start()
            self._copy(self.step, 1).start()

    def before_gather(self, t):
        if t % INDEX_SLOT_TOKENS == 0:
            self._copy(self.step, t // INDEX_SLOT_TOKENS).wait()

    def after_gather(self, t):
        if t % INDEX_SLOT_TOKENS == INDEX_SLOT_TOKENS - 1:
            self._copy(jnp.minimum(self.step + 1, self.last), t // INDEX_SLOT_TOKENS).start()

    def drain(self):
        @pl.when(self.step == self.last)
        def _():
            self._copy(self.step, 0).wait()
            self._copy(self.step, 1).wait()

    def __call__(self, t, j):
        return self.idx_ref[t // INDEX_SLOT_TOKENS, t % INDEX_SLOT_TOKENS, j]


def _index_scratch():
    return [pltpu.SMEM((2, INDEX_SLOT_TOKENS, PEER_PAIRS), jnp.int32), pltpu.SemaphoreType.DMA((2,))]


def _gather_slots(index, tab_ref, t, scale=None):
    index.before_gather(t)

    def expert(j):
        row0 = pl.multiple_of(index(t, j), EXPERT_WORDS)
        return tab_ref[pl.ds(row0, EXPERT_WORDS), :]

    def tile(j):
        words = jnp.concatenate([expert(j), expert(j + 1)], axis=0)
        x = pltpu.bitcast(words, BF16)
        return x if scale is None else x * scale

    blocks = []
    for k in range(SLOTS // SLOTS_PER_MXU_TILE):
        slots = [jnp.concatenate([tile(2 * i), tile(PEER_PAIRS // 2 + 2 * i)], axis=1)
                 for i in range(k * SLOTS_PER_MXU_TILE, (k + 1) * SLOTS_PER_MXU_TILE)]
        blocks.append(jnp.concatenate(slots, axis=0))
    index.after_gather(t)
    return blocks


def _peer_u_kernel(idx_hbm, h_ref, gate_ref, half_ref, tab_ref, act_ref, idx_ref, sem_ref):
    tb = h_ref.shape[0]
    ones = jnp.ones((BF16_SUBLANES, LANES), BF16)
    half_sum = half_ref[...]
    index = _IndexSlots(idx_hbm, idx_ref, sem_ref)
    index.prime()

    def gather(t):
        ht = h_ref[t].astype(BF16)
        return _gather_slots(index, tab_ref, t, scale=jnp.concatenate([ht, ht], axis=0))

    def half_sums(prod):
        z = [_dot(half_sum, block) for block in prod]
        return jnp.concatenate(z, axis=0).astype(BF16)

    def lane_sums(t, z):
        s = _dot_nt(ones, jnp.concatenate([z[:, :LANES], z[:, LANES:]], axis=0))
        act_ref[t:t + 1, :] = s[0:1, :]

    prod = gather(0)
    pending = None
    for t in range(tb):
        nxt = gather(t + 1) if t + 1 < tb else None
        z = half_sums(prod)
        if pending is not None:
            lane_sums(t - 1, pending)
        pending, prod = z, nxt
    lane_sums(tb - 1, pending)
    index.drain()
    a = act_ref[...]
    act_ref[...] = 0.5 * a * (1.0 + lax.erf(a * (2.0 ** -0.5))) * gate_ref[...]


def _peer_u(idx, h3, gate, tab, tb):
    t = idx.shape[0]
    r = jnp.arange(2 * SLOTS_PER_MXU_TILE)[:, None]
    c = jnp.arange(MXU_ROWS)[None, :]
    half_sum = ((c // TILE_ROWS == r // 2) & ((c % TILE_ROWS) // ROWS_PER_EXPERT == r % 2)).astype(BF16)
    row = pl.BlockSpec((tb, PEER_PAIRS), lambda i: (i, 0))
    return pl.pallas_call(
        _peer_u_kernel,
        grid=(t // tb,),
        in_specs=[
            pl.BlockSpec(memory_space=pl.ANY),
            pl.BlockSpec((tb, ROWS_PER_EXPERT, LANES), lambda i: (i, 0, 0)),
            row,
            pl.BlockSpec(half_sum.shape, lambda i: (0, 0)),
            pl.BlockSpec(memory_space=pltpu.VMEM),
        ],
        out_specs=row,
        out_shape=jax.ShapeDtypeStruct((t, PEER_PAIRS), F32),
        scratch_shapes=_index_scratch(),
        compiler_params=pltpu.CompilerParams(
            dimension_semantics=("arbitrary",), vmem_limit_bytes=VMEM_LIMIT_TABLE),
        name="peer_u",
    )(idx, h3, gate, half_sum, tab)


def _peer_v_kernel(idx_hbm, act_ref, exp_ref, tab_ref, y_ref, coef_ref, idx_ref, sem_ref):
    tb = act_ref.shape[0]
    n = ROWS_PER_EXPERT
    index = _IndexSlots(idx_hbm, idx_ref, sem_ref)
    index.prime()
    act = act_ref[...]
    hi = act.astype(BF16)
    lhs = jnp.concatenate([hi, (act - hi.astype(F32)).astype(BF16)], axis=0)
    lhs = jnp.broadcast_to(lhs[:, None, :], (2 * tb, n, PEER_PAIRS)).reshape(2 * tb * n, PEER_PAIRS)
    coef_ref[...] = _dot(lhs, exp_ref[...])
    shape = (n, SLOT_ROWS)
    own_row = (lax.broadcasted_iota(jnp.int32, shape, 1) % n) == lax.broadcasted_iota(jnp.int32, shape, 0)

    def consume(t, xs):
        def part(kind, side):
            r0 = (kind * tb + t) * n
            return jnp.where(own_row, coef_ref[r0:r0 + n, side * SLOT_ROWS:(side + 1) * SLOT_ROWS], 0.0)

        coef = jnp.concatenate([part(0, 0), part(0, 1), part(1, 0), part(1, 1)], axis=0).astype(BF16)
        res = _dot(coef, jnp.concatenate(xs, axis=0))
        y_ref[t] =((res[0:n, :LANES] + res[2 * n:3 * n, :LANES])
                    + (res[n:2 * n, LANES:] + res[3 * n:4 * n, LANES:]))

    xs = _gather_slots(index, tab_ref, 0)
    for t in range(tb):
        nxt = _gather_slots(index, tab_ref, t + 1) if t + 1 < tb else None
        consume(t, xs)
        xs = nxt
    index.drain()


def _peer_v(idx, act, tab, tb):
    t = idx.shape[0]
    j = jnp.arange(PEER_PAIRS)[:, None]
    c = jnp.arange(2 * SLOT_ROWS)[None, :]
    pair_of_col = ((PEER_PAIRS // 2) * (c // SLOT_ROWS) + 2 * ((c % SLOT_ROWS) // TILE_ROWS)
                   + (c % TILE_ROWS) // ROWS_PER_EXPERT)
    expand = (pair_of_col == j).astype(BF16)
    row = pl.BlockSpec((tb, PEER_PAIRS), lambda i: (i, 0))
    return pl.pallas_call(
        _peer_v_kernel,
        grid=(t // tb,),
        in_specs=[
            pl.BlockSpec(memory_space=pl.ANY),
            row,
            pl.BlockSpec(expand.shape, lambda i: (0, 0)),
            pl.BlockSpec(memory_space=pltpu.VMEM),
        ],
        out_specs=pl.BlockSpec((tb, ROWS_PER_EXPERT, LANES), lambda i: (i, 0, 0)),
        out_shape=jax.ShapeDtypeStruct((t, ROWS_PER_EXPERT, LANES), F32),
        scratch_shapes=[pltpu.VMEM((2 * tb * ROWS_PER_EXPERT, 2 * SLOT_ROWS), F32)] + _index_scratch(),
        compiler_params=pltpu.CompilerParams(
            dimension_semantics=("arbitrary",), vmem_limit_bytes=VMEM_LIMIT_TABLE),
        name="peer_v",
    )(idx, act, expand, tab)


def _ple_kernel(x1_ref, y_ref, p_ref, g_ref, wg_ref, wp_ref, o_ref):
    y = jnp.concatenate([y_ref[:, r, :] for r in range(ROWS_PER_EXPERT)], axis=1)
    x2 = x1_ref[...] + y
    hg = _rms(x2, g_ref[...])
    gate = jax.nn.sigmoid(_dot(hg.astype(BF16), wg_ref[...]))
    o_ref[...] = x2 + gate * _dot(p_ref[...].astype(BF16), wp_ref[...])


def _ple(x1, y, p2d, g, wg, wp, tm):
    t, d = x1.shape
    row = lambda w: pl.BlockSpec((tm, w), lambda i: (i, 0))
    full = lambda a: pl.BlockSpec(a.shape, lambda i: (0,) * a.ndim)
    return pl.pallas_call(
        _ple_kernel,
        grid=(t // tm,),
        in_specs=[row(d), pl.BlockSpec((tm, ROWS_PER_EXPERT, LANES), lambda i: (i, 0, 0)), row(p2d.shape[1]),
                  full(g), full(wg), full(wp)],
        out_specs=row(d),
        out_shape=jax.ShapeDtypeStruct((t, d), F32),
        compiler_params=pltpu.CompilerParams(
            dimension_semantics=("parallel",), vmem_limit_bytes=VMEM_LIMIT_DENSE),
        name="ple",
    )(x1, y, p2d, g, wg, wp)


def _expert_table_kernel(emb_ref, tab_ref):
    ne = emb_ref.shape[0]

    def bf16_bits(x):
        return lax.bitcast_convert_type(x.astype(BF16).astype(F32), jnp.uint32)

    for s in range(EXPERT_WORDS):
        lo = bf16_bits(emb_ref[:, (2 * s) * LANES:(2 * s + 1) * LANES])
        hi = bf16_bits(emb_ref[:, (2 * s + 1) * LANES:(2 * s + 2) * LANES])
        tab_ref[pl.ds(s, ne, stride=EXPERT_WORDS), :] = (lo >> 16) | hi


def _expert_table(emb):
    n, d = emb.shape
    assert d == ROWS_PER_EXPERT * LANES
    ne = 256
    return pl.pallas_call(
        _expert_table_kernel,
        grid=(n // ne,),
        in_specs=[pl.BlockSpec((ne, d), lambda i: (i, 0))],
        out_specs=pl.BlockSpec((ne * EXPERT_WORDS, LANES), lambda i: (i, 0)),
        out_shape=jax.ShapeDtypeStruct((n * EXPERT_WORDS, LANES), jnp.uint32),
        compiler_params=pltpu.CompilerParams(dimension_semantics=("parallel",)),
        name="expert_table",
    )(emb)


def _layer(x, p, g_mix, w_in, q_norm, k_norm, sinks, w_out, g_ffn, w_query, sub_keys, emb_u, emb_v,
           g_ple, w_gate, w_proj):
    b, s, d = x.shape
    t = b * s
    tm = 256 if t % 256 == 0 else CHUNK
    tb = PEER_TOKEN_BLOCK
    assert t % tb == 0
    x2d = x.reshape(t, d)
    proj = _in_proj(x2d, g_mix[None, :], w_in.astype(BF16), tm)
    proj3 = proj.reshape(b, s, proj.shape[1])
    ro = _retention(proj3, _retention_tables(s))
    so = _swa(proj3, sinks, q_norm[None, :], k_norm[None, :])
    wo = w_out.astype(BF16)
    x1, h3, qp = _out_proj(x2d, ro.reshape(t, RET_WIDTH), so.reshape(t, SWA_Q_WIDTH),
                           wo[:RET_WIDTH], wo[RET_WIDTH:], g_ffn[None, :], w_query.astype(BF16), tm)
    idx, gate = _peer_topk(qp, sub_keys.astype(BF16), tm)
    act = _peer_u(idx, h3, gate, _expert_table(emb_u), tb)
    y = _peer_v(idx, act, _expert_table(emb_v), tb)
    out = _ple(x1, y, p.reshape(t, p.shape[-1]), g_ple[None, :],
               w_gate.astype(BF16), w_proj.astype(BF16), tm)
    return out.reshape(b, s, d)


def kernel(x, p, g_mix, w_in, q_norm, k_norm, sinks, w_out, g_ffn, peer_w_query, peer_sub_keys,
           peer_u, peer_v, g_ple, w_ple_gate, w_ple_proj):
    for i in range(p.shape[0]):
        x = _layer(x, p[i], g_mix[i], w_in[i], q_norm[i], k_norm[i], sinks[i], w_out[i], g_ffn[i],
                   peer_w_query[i], peer_sub_keys[i], peer_u[i], peer_v[i], g_ple[i], w_ple_gate[i],
                   w_ple_proj[i])
    return x
```

```python
import functools

import jax
import jax.numpy as jnp
from jax import lax
from jax.experimental import pallas as pl
from jax.experimental.pallas import tpu as pltpu

F32 = jnp.float32
BF16 = jnp.bfloat16

HEAD_DIM = 64
RET_HEADS = 8
SWA_Q_HEADS = 8
SWA_KV_HEADS = 2
SWA_GROUP = SWA_Q_HEADS // SWA_KV_HEADS
RET_WIDTH = RET_HEADS * HEAD_DIM
SWA_Q_WIDTH = SWA_Q_HEADS * HEAD_DIM
SWA_KV_WIDTH = SWA_KV_HEADS * HEAD_DIM
CHUNK = 128
ROPE_BASE = 10000.0
PEER_HEADS = 8
PEER_KEYS = 128
PEER_HALF = 128
PEER_TOPK = 16
PEER_PAIRS = PEER_HEADS * PEER_TOPK
EPS = 1e-6

LANES = 128
F32_SUBLANES = 8
BF16_SUBLANES = 16
VMEM_LIMIT_DENSE = 48 * 1024 * 1024
VMEM_LIMIT_TABLE = 56 * 1024 * 1024

ROWS_PER_EXPERT = 1024 // LANES
TILE_ROWS = BF16_SUBLANES


def _rms(x, g):
    ms = jnp.mean(x * x, axis=-1, keepdims=True)
    return x * lax.rsqrt(ms + EPS) * g


def _dot(a, b):
    return jnp.dot(a, b, preferred_element_type=F32)


def _dot_nt(a, b):
    return lax.dot_general(a, b, (((1,), (1,)), ((), ())), preferred_element_type=F32)


def _dot_tn(a, b):
    return lax.dot_general(a, b, (((0,), (0,)), ((), ())), preferred_element_type=F32)


def _in_proj_kernel(x_ref, g_ref, w_ref, o_ref):
    h = _rms(x_ref[...], g_ref[...])
    o_ref[...] = _dot(h.astype(BF16), w_ref[...])


def _in_proj(x2d, g, w, tm):
    t, d = x2d.shape
    n = w.shape[1]
    return pl.pallas_call(
        _in_proj_kernel,
        grid=(t // tm,),
        in_specs=[
            pl.BlockSpec((tm, d), lambda i: (i, 0)),
            pl.BlockSpec((1, d), lambda i: (0, 0)),
            pl.BlockSpec((d, n), lambda i: (0, 0)),
        ],
        out_specs=pl.BlockSpec((tm, n), lambda i: (i, 0)),
        out_shape=jax.ShapeDtypeStruct((t, n), F32),
        compiler_params=pltpu.CompilerParams(
            dimension_semantics=("parallel",), vmem_limit_bytes=VMEM_LIMIT_DENSE),
        name="in_proj",
    )(x2d, g, w)


def _retention_kernel(cd_ref, q_ref, k_ref, v_ref, g_ref, cos_ref, sin_ref, dec_ref, wk_ref, wq_ref,
                      o_ref, state_ref):
    n = pl.program_id(1)

    @pl.when(n == 0)
    def _():
        state_ref[...] = jnp.zeros_like(state_ref)

    c = cos_ref[...]
    s = sin_ref[...]
    lane = lax.broadcasted_iota(jnp.int32, (CHUNK, RET_WIDTH), 1)
    first_half = (lane % HEAD_DIM) < (HEAD_DIM // 2)

    def rot(x):
        partner = jnp.where(first_half,
                            pltpu.roll(x, RET_WIDTH - HEAD_DIM // 2, 1),
                            pltpu.roll(x, HEAD_DIM // 2, 1))
        return x * c + partner * s

    wq = wq_ref[...]
    for bb in range(q_ref.shape[0]):
        q = rot(q_ref[bb])
        k = rot(k_ref[bb]) * (HEAD_DIM ** -0.5)
        kw = k * wk_ref[...]
        v = v_ref[bb]
        gate = g_ref[bb]
        for h in range(RET_HEADS):
            sl = slice(h * HEAD_DIM, (h + 1) * HEAD_DIM)
            qh = q[:, sl].astype(BF16)
            kh = k[:, sl].astype(BF16)
            vh = v[:, sl].astype(BF16)
            att = _dot_nt(qh, kh) * dec_ref[h]
            o = _dot(att.astype(BF16), vh)
            st = state_ref[bb, h]
            o = o + _dot(qh, st.astype(BF16)) * wq[:, sl]
            state_ref[bb, h] = st * cd_ref[h] + _dot_tn(kw[:, sl].astype(BF16), vh)
            mu = jnp.mean(o, axis=-1, keepdims=True)
            oc = o - mu
            var = jnp.mean(oc * oc, axis=-1, keepdims=True)
            on = oc * lax.rsqrt(var + EPS)
            gh = gate[:, sl]
            o_ref[bb, :, sl] = gh * jax.nn.sigmoid(gh) * on


def _retention(proj3, tables):
    b, s, _ = proj3.shape
    cd, cos_t, sin_t, dec, wk, wq = tables
    nb = 1
    col = lambda j: pl.BlockSpec((nb, CHUNK, RET_WIDTH), lambda bi, ni, j=j: (bi, ni, j))
    pos_spec = pl.BlockSpec((CHUNK, RET_WIDTH), lambda bi, ni: (ni, 0))
    const2 = pl.BlockSpec((CHUNK, RET_WIDTH), lambda bi, ni: (0, 0))
    return pl.pallas_call(
        _retention_kernel,
        grid=(b // nb, s // CHUNK),
        in_specs=[
            pl.BlockSpec(memory_space=pltpu.SMEM),
            col(0), col(1), col(2), col(3),
            pos_spec, pos_spec,
            pl.BlockSpec((RET_HEADS, CHUNK, CHUNK), lambda bi, ni: (0, 0, 0)),
            const2, const2,
        ],
        out_specs=pl.BlockSpec((nb, CHUNK, RET_WIDTH), lambda bi, ni: (bi, ni, 0)),
        out_shape=jax.ShapeDtypeStruct((b, s, RET_WIDTH), F32),
        scratch_shapes=[pltpu.VMEM((nb, RET_HEADS, HEAD_DIM, HEAD_DIM), F32)],
        compiler_params=pltpu.CompilerParams(
            dimension_semantics=("parallel", "arbitrary"), vmem_limit_bytes=VMEM_LIMIT_DENSE),
        name="retention",
    )(cd, proj3, proj3, proj3, proj3, cos_t, sin_t, dec, wk, wq)


def _retention_tables(s):
    half = HEAD_DIM // 2
    pos = jnp.arange(s, dtype=F32)
    freqs = ROPE_BASE ** (-jnp.arange(half, dtype=F32) / half)
    ang = pos[:, None] * freqs[None, :]
    c = jnp.cos(ang)
    sn = jnp.sin(ang)
    cos_t = jnp.tile(jnp.concatenate([c, c], axis=-1), (1, RET_HEADS))
    sin_t = jnp.tile(jnp.concatenate([-sn, sn], axis=-1), (1, RET_HEADS))
    gamma = 1.0 - jnp.exp2(-5.0 - jnp.arange(RET_HEADS, dtype=F32))
    log_g = jnp.log(gamma)
    idx = jnp.arange(CHUNK)
    rel = idx[:, None] - idx[None, :]
    dec = jnp.where(rel[None] >= 0,
                    jnp.exp(log_g[:, None, None] * jnp.maximum(rel, 0)[None].astype(F32)),
                    0.0)
    w_k = jnp.exp(log_g[None, :] * (CHUNK - 1 - idx)[:, None].astype(F32))
    w_q = jnp.exp(log_g[None, :] * (idx + 1)[:, None].astype(F32))
    cd = jnp.exp(log_g * CHUNK)
    wk = jnp.repeat(w_k, HEAD_DIM, axis=1)
    wq = jnp.repeat(w_q, HEAD_DIM, axis=1)
    return cd, cos_t, sin_t, dec, wk, wq


def _swa_kernel(sink_ref, q_ref, kc_ref, vc_ref, kp_ref, vp_ref, qn_ref, kn_ref, o_ref):
    n = pl.program_id(1)
    qi = lax.broadcasted_iota(jnp.int32, (CHUNK, 2 * CHUNK), 0)
    ki = lax.broadcasted_iota(jnp.int32, (CHUNK, 2 * CHUNK), 1)
    rel = qi + CHUNK - ki
    mask = (rel >= 0) & (rel < CHUNK) & ((n > 0) | (ki >= CHUNK))
    qn = qn_ref[...]
    kn = kn_ref[...]
    for bb in range(q_ref.shape[0]):
        q = q_ref[bb]
        for g in range(SWA_KV_HEADS):
            sl = slice(g * HEAD_DIM, (g + 1) * HEAD_DIM)
            kcat = jnp.concatenate([kp_ref[bb, :, sl], kc_ref[bb, :, sl]], axis=0)
            kcat = _rms(kcat, kn).astype(BF16)
            vcat = jnp.concatenate([vp_ref[bb, :, sl], vc_ref[bb, :, sl]], axis=0).astype(BF16)
            for j in range(SWA_GROUP):
                h = g * SWA_GROUP + j
                hs = slice(h * HEAD_DIM, (h + 1) * HEAD_DIM)
                qh = _rms(q[:, hs], qn).astype(BF16)
                sc = _dot_nt(qh, kcat) * (HEAD_DIM ** -0.5)
                sc = jnp.where(mask, sc, -jnp.inf)
                sink = sink_ref[h]
                m = jnp.maximum(jnp.max(sc, axis=-1, keepdims=True), sink)
                e = jnp.exp(sc - m)
                pr = e / (jnp.sum(e, axis=-1, keepdims=True) + jnp.exp(sink - m))
                o_ref[bb, :, hs] = _dot(pr.astype(BF16), vcat)


def _swa(proj3, sinks, qn, kn):
    b, s, _ = proj3.shape
    q_blk = (4 * RET_WIDTH) // SWA_Q_WIDTH
    k_blk = (4 * RET_WIDTH + SWA_Q_WIDTH) // SWA_KV_WIDTH
    v_blk = k_blk + 1
    nb = 1
    cur = lambda j: pl.BlockSpec((nb, CHUNK, SWA_KV_WIDTH), lambda bi, ni, j=j: (bi, ni, j))
    prev = lambda j: pl.BlockSpec((nb, CHUNK, SWA_KV_WIDTH),
                                  lambda bi, ni, j=j: (bi, jnp.maximum(ni - 1, 0), j))
    return pl.pallas_call(
        _swa_kernel,
        grid=(b // nb, s // CHUNK),
        in_specs=[
            pl.BlockSpec(memory_space=pltpu.SMEM),
            pl.BlockSpec((nb, CHUNK, SWA_Q_WIDTH), lambda bi, ni: (bi, ni, q_blk)),
            cur(k_blk), cur(v_blk), prev(k_blk), prev(v_blk),
            pl.BlockSpec((1, HEAD_DIM), lambda bi, ni: (0, 0)),
            pl.BlockSpec((1, HEAD_DIM), lambda bi, ni: (0, 0)),
        ],
        out_specs=pl.BlockSpec((nb, CHUNK, SWA_Q_WIDTH), lambda bi, ni: (bi, ni, 0)),
        out_shape=jax.ShapeDtypeStruct((b, s, SWA_Q_WIDTH), F32),
        compiler_params=pltpu.CompilerParams(
            dimension_semantics=("parallel", "parallel"), vmem_limit_bytes=VMEM_LIMIT_DENSE),
        name="swa",
    )(sinks, proj3, proj3, proj3, proj3, proj3, qn, kn)


def _out_proj_kernel(x_ref, ro_ref, so_ref, wo1_ref, wo2_ref, g_ref, wq_ref, x1_ref, h2_ref, qp_ref):
    x1 = (x_ref[...] + _dot(ro_ref[...].astype(BF16), wo1_ref[...])
          + _dot(so_ref[...].astype(BF16), wo2_ref[...]))
    x1_ref[...] = x1
    h2 = _rms(x1, g_ref[...])
    for r in range(ROWS_PER_EXPERT):
        h2_ref[:, r, :] = h2[:, r * LANES:(r + 1) * LANES]
    qp = _dot(h2.astype(BF16), wq_ref[...])
    width = qp_ref.shape[2]
    for h in range(PEER_HEADS):
        qp_ref[h] = qp[:, h * width:(h + 1) * width]


def _out_proj(x2d, ro, so, wo1, wo2, g, wq, tm):
    t, d = x2d.shape
    nq = wq.shape[1]
    width = nq // PEER_HEADS
    row = lambda w: pl.BlockSpec((tm, w), lambda i: (i, 0))
    full = lambda a: pl.BlockSpec(a.shape, lambda i: (0,) * a.ndim)
    return pl.pallas_call(
        _out_proj_kernel,
        grid=(t // tm,),
        in_specs=[row(d), row(ro.shape[1]), row(so.shape[1]), full(wo1), full(wo2), full(g), full(wq)],
        out_specs=[row(d), pl.BlockSpec((tm, ROWS_PER_EXPERT, LANES), lambda i: (i, 0, 0)),
                   pl.BlockSpec((PEER_HEADS, tm, width), lambda i: (0, i, 0))],
        out_shape=[jax.ShapeDtypeStruct((t, d), F32), jax.ShapeDtypeStruct((t, ROWS_PER_EXPERT, LANES), F32),
                   jax.ShapeDtypeStruct((PEER_HEADS, t, width), F32)],
        compiler_params=pltpu.CompilerParams(
            dimension_semantics=("parallel",), vmem_limit_bytes=VMEM_LIMIT_DENSE),
        name="out_proj",
    )(x2d, ro, so, wo1, wo2, g, wq)


_CAND_B = (16, 8, 8, 8, 8, 8, 8, 8)
TOPK_HEADS_PER_STEP = 8


def _candidate_flat_index():
    pos = [a * PEER_TOPK + b for a, nb in enumerate(_CAND_B) for b in range(nb)]
    pos += [a * PEER_TOPK for a in range(len(_CAND_B), PEER_TOPK)]
    return pos


def _peer_topk_kernel(qp_ref, keys_ref, pos_ref, idx_ref, gate_ref,
                      s_ref, i_ref, best_ref, exp_ref, et_ref, gt_ref):
    g = pl.program_id(1)
    heads, tm = qp_ref.shape[0], qp_ref.shape[1]
    neg = -jnp.inf
    key_id = lax.broadcasted_iota(jnp.int32, (PEER_KEYS, tm), 0).astype(F32)
    pos = pos_ref[...]
    for hh in range(heads):
        for p in range(2):
            qhp = qp_ref[hh, :, p * PEER_HALF:(p + 1) * PEER_HALF].astype(BF16)
            sc = _dot_nt(keys_ref[p], qhp)
            for a in range(PEER_TOPK):
                m = jnp.max(sc, axis=0, keepdims=True)
                i = jnp.min(jnp.where(sc == m, key_id, float(PEER_KEYS)), axis=0, keepdims=True)
                sc = jnp.where(key_id == i, neg, sc)
                s_ref[hh, p, a:a + 1, :] = m
                i_ref[hh, p, a:a + 1, :] = i
    for hh in range(heads):
        s0, s1 = s_ref[hh, 0], s_ref[hh, 1]
        i0, i1 = i_ref[hh, 0] * float(PEER_KEYS), i_ref[hh, 1]
        cs_blocks, ci_blocks = [], []
        for a, nb in enumerate(_CAND_B):
            cs_blocks.append(s0[a:a + 1] + s1[0:nb])
            ci_blocks.append(i0[a:a + 1] + i1[0:nb])
        na = len(_CAND_B)
        cs_blocks.append(s0[na:] + s1[0:1])
        ci_blocks.append(i0[na:] + i1[0:1])
        cs = jnp.concatenate(cs_blocks, axis=0)
        ci = jnp.concatenate(ci_blocks, axis=0)
        for k in range(PEER_TOPK):
            m = jnp.max(cs, axis=0, keepdims=True)
            first = jnp.min(jnp.where(cs == m, pos, float(PEER_TOPK * PEER_TOPK)), axis=0, keepdims=True)
            sel = pos == first
            best_ref[hh, k:k + 1, :] = m
            exp_ref[hh, k:k + 1, :] = jnp.max(jnp.where(sel, ci, -1.0), axis=0, keepdims=True)
            cs = jnp.where(sel, neg, cs)
        best = best_ref[hh]
        ex = jnp.exp(best - best[0:1])
        gate = ex / jnp.sum(ex, axis=0, keepdims=True)
        rows = pl.ds(pl.multiple_of((g * heads + hh) * PEER_TOPK, PEER_TOPK), PEER_TOPK)
        et_ref[rows, :] = exp_ref[hh]
        gt_ref[rows, :] = gate

    @pl.when(g == pl.num_programs(1) - 1)
    def _():
        idx_ref[...] = et_ref[...].T.astype(jnp.int32) * EXPERT_WORDS
        gate_ref[...] = gt_ref[...].T


def _peer_topk(qp, keys, tm):
    nh, t, width = qp.shape
    hps = TOPK_HEADS_PER_STEP
    pos = _candidate_flat_index()
    pos = jnp.broadcast_to(jnp.asarray(pos, F32)[:, None], (len(pos), tm))
    full = lambda a: pl.BlockSpec(a.shape, lambda i, h: (0,) * a.ndim)
    out_spec = pl.BlockSpec((tm, PEER_PAIRS), lambda i, h: (i, 0))
    return pl.pallas_call(
        _peer_topk_kernel,
        grid=(t // tm, nh // hps),
        in_specs=[pl.BlockSpec((hps, tm, width), lambda i, h: (h, i, 0)), full(keys), full(pos)],
        out_specs=[out_spec, out_spec],
        out_shape=[jax.ShapeDtypeStruct((t, PEER_PAIRS), jnp.int32),
                   jax.ShapeDtypeStruct((t, PEER_PAIRS), F32)],
        scratch_shapes=[pltpu.VMEM((hps, 2, PEER_TOPK, tm), F32), pltpu.VMEM((hps, 2, PEER_TOPK, tm), F32),
                        pltpu.VMEM((hps, PEER_TOPK, tm), F32), pltpu.VMEM((hps, PEER_TOPK, tm), F32),
                        pltpu.VMEM((PEER_PAIRS, tm), F32), pltpu.VMEM((PEER_PAIRS, tm), F32)],
        compiler_params=pltpu.CompilerParams(
            dimension_semantics=("parallel", "arbitrary"), vmem_limit_bytes=VMEM_LIMIT_DENSE),
        name="peer_topk",
    )(qp, keys, pos)


EXPERT_WORDS = ROWS_PER_EXPERT // 2
SLOTS = PEER_PAIRS // 4
SLOT_ROWS = SLOTS * TILE_ROWS
MXU_ROWS = 256
SLOTS_PER_MXU_TILE = MXU_ROWS // TILE_ROWS
INDEX_SLOT_TOKENS = 32
PEER_TOKEN_BLOCK = 2 * INDEX_SLOT_TOKENS


class _IndexSlots:
    def __init__(self, idx_hbm, idx_ref, sem_ref):
        self.idx_hbm, self.idx_ref, self.sem_ref = idx_hbm, idx_ref, sem_ref
        self.step = pl.program_id(0)
        self.last = pl.num_programs(0) - 1

    def _copy(self, step, slot):
        row0 = pl.multiple_of((2 * step + slot) * INDEX_SLOT_TOKENS, INDEX_SLOT_TOKENS)
        return pltpu.make_async_copy(self.idx_hbm.at[pl.ds(row0, INDEX_SLOT_TOKENS)],
                                     self.idx_ref.at[slot], self.sem_ref.at[slot])

    def prime(self):
        @pl.when(self.step == 0)
        def _():
            self._copy(self.step, 0).start()
            self._copy(self.step, 1).start()

    def before_gather(self, t):
        if t % INDEX_SLOT_TOKENS == 0:
            self._copy(self.step, t // INDEX_SLOT_TOKENS).wait()

    def after_gather(self, t):
        if t % INDEX_SLOT_TOKENS == INDEX_SLOT_TOKENS - 1:
            self._copy(jnp.minimum(self.step + 1, self.last), t // INDEX_SLOT_TOKENS).start()

    def drain(self):
        @pl.when(self.step == self.last)
        def _():
            self._copy(self.step, 0).wait()
            self._copy(self.step, 1).wait()

    def __call__(self, t, j):
        return self.idx_ref[t // INDEX_SLOT_TOKENS, t % INDEX_SLOT_TOKENS, j]


def _index_scratch():
    return [pltpu.SMEM((2, INDEX_SLOT_TOKENS, PEER_PAIRS), jnp.int32), pltpu.SemaphoreType.DMA((2,))]


def _gather_slots(index, tab_ref, t, scale=None):
    index.before_gather(t)

    def expert(j):
        row0 = pl.multiple_of(index(t, j), EXPERT_WORDS)
        return tab_ref[pl.ds(row0, EXPERT_WORDS), :]

    def tile(j):
        words = jnp.concatenate([expert(j), expert(j + 1)], axis=0)
        x = pltpu.bitcast(words, BF16)
        return x if scale is None else x * scale

    blocks = []
    for k in range(SLOTS // SLOTS_PER_MXU_TILE):
        slots = [jnp.concatenate([tile(2 * i), tile(PEER_PAIRS // 2 + 2 * i)], axis=1)
                 for i in range(k * SLOTS_PER_MXU_TILE, (k + 1) * SLOTS_PER_MXU_TILE)]
        blocks.append(jnp.concatenate(slots, axis=0))
    index.after_gather(t)
    return blocks


def _peer_u_kernel(idx_hbm, h_ref, gate_ref, half_ref, tab_ref, act_ref, idx_ref, sem_ref):
    tb = h_ref.shape[0]
    ones = jnp.ones((BF16_SUBLANES, LANES), BF16)
    half_sum = half_ref[...]
    index = _IndexSlots(idx_hbm, idx_ref, sem_ref)
    index.prime()

    def gather(t):
        ht = h_ref[t].astype(BF16)
        return _gather_slots(index, tab_ref, t, scale=jnp.concatenate([ht, ht], axis=0))

    def half_sums(prod):
        z = [_dot(half_sum, block) for block in prod]
        return jnp.concatenate(z, axis=0).astype(BF16)

    def lane_sums(t, z):
        s = _dot_nt(ones, jnp.concatenate([z[:, :LANES], z[:, LANES:]], axis=0))
        act_ref[t:t + 1, :] = s[0:1, :]

    prod = gather(0)
    pending = None
    for t in range(tb):
        nxt = gather(t + 1) if t + 1 < tb else None
        z = half_sums(prod)
        if pending is not None:
            lane_sums(t - 1, pending)
        pending, prod = z, nxt
    lane_sums(tb - 1, pending)
    index.drain()
    a = act_ref[...]
    act_ref[...] = 0.5 * a * (1.0 + lax.erf(a * (2.0 ** -0.5))) * gate_ref[...]


def _peer_u(idx, h3, gate, tab, tb):
    t = idx.shape[0]
    r = jnp.arange(2 * SLOTS_PER_MXU_TILE)[:, None]
    c = jnp.arange(MXU_ROWS)[None, :]
    half_sum = ((c // TILE_ROWS == r // 2) & ((c % TILE_ROWS) // ROWS_PER_EXPERT == r % 2)).astype(BF16)
    row = pl.BlockSpec((tb, PEER_PAIRS), lambda i: (i, 0))
    return pl.pallas_call(
        _peer_u_kernel,
        grid=(t // tb,),
        in_specs=[
            pl.BlockSpec(memory_space=pl.ANY),
            pl.BlockSpec((tb, ROWS_PER_EXPERT, LANES), lambda i: (i, 0, 0)),
            row,
            pl.BlockSpec(half_sum.shape, lambda i: (0, 0)),
            pl.BlockSpec(memory_space=pltpu.VMEM),
        ],
        out_specs=row,
        out_shape=jax.ShapeDtypeStruct((t, PEER_PAIRS), F32),
        scratch_shapes=_index_scratch(),
        compiler_params=pltpu.CompilerParams(
            dimension_semantics=("arbitrary",), vmem_limit_bytes=VMEM_LIMIT_TABLE),
        name="peer_u",
    )(idx, h3, gate, half_sum, tab)


def _peer_v_kernel(idx_hbm, act_ref, exp_ref, tab_ref, y_ref, coef_ref, idx_ref, sem_ref):
    tb = act_ref.shape[0]
    n = ROWS_PER_EXPERT
    index = _IndexSlots(idx_hbm, idx_ref, sem_ref)
    index.prime()
    act = act_ref[...]
    hi = act.astype(BF16)
    lhs = jnp.concatenate([hi, (act - hi.astype(F32)).astype(BF16)], axis=0)
    lhs = jnp.broadcast_to(lhs[:, None, :], (2 * tb, n, PEER_PAIRS)).reshape(2 * tb * n, PEER_PAIRS)
    coef_ref[...] = _dot(lhs, exp_ref[...])
    shape = (n, SLOT_ROWS)
    own_row = (lax.broadcasted_iota(jnp.int32, shape, 1) % n) == lax.broadcasted_iota(jnp.int32, shape, 0)

    def consume(t, xs):
        def part(kind, side):
            r0 = (kind * tb + t) * n
            return jnp.where(own_row, coef_ref[r0:r0 + n, side * SLOT_ROWS:(side + 1) * SLOT_ROWS], 0.0)

        coef = jnp.concatenate([part(0, 0), part(0, 1), part(1, 0), part(1, 1)], axis=0).astype(BF16)
        res = _dot(coef, jnp.concatenate(xs, axis=0))
        y_ref[t] =((res[0:n, :LANES] + res[2 * n:3 * n, :LANES])
                    + (res[n:2 * n, LANES:] + res[3 * n:4 * n, LANES:]))

    xs = _gather_slots(index, tab_ref, 0)
    for t in range(tb):
        nxt = _gather_slots(index, tab_ref, t + 1) if t + 1 < tb else None
        consume(t, xs)
        xs = nxt
    index.drain()


def _peer_v(idx, act, tab, tb):
    t = idx.shape[0]
    j = jnp.arange(PEER_PAIRS)[:, None]
    c = jnp.arange(2 * SLOT_ROWS)[None, :]
    pair_of_col = ((PEER_PAIRS // 2) * (c // SLOT_ROWS) + 2 * ((c % SLOT_ROWS) // TILE_ROWS)
                   + (c % TILE_ROWS) // ROWS_PER_EXPERT)
    expand = (pair_of_col == j).astype(BF16)
    row = pl.BlockSpec((tb, PEER_PAIRS), lambda i: (i, 0))
    return pl.pallas_call(
        _peer_v_kernel,
        grid=(t // tb,),
        in_specs=[
            pl.BlockSpec(memory_space=pl.ANY),
            row,
            pl.BlockSpec(expand.shape, lambda i: (0, 0)),
            pl.BlockSpec(memory_space=pltpu.VMEM),
        ],
        out_specs=pl.BlockSpec((tb, ROWS_PER_EXPERT, LANES), lambda i: (i, 0, 0)),
        out_shape=jax.ShapeDtypeStruct((t, ROWS_PER_EXPERT, LANES), F32),
        scratch_shapes=[pltpu.VMEM((2 * tb * ROWS_PER_EXPERT, 2 * SLOT_ROWS), F32)] + _index_scratch(),
        compiler_params=pltpu.CompilerParams(
            dimension_semantics=("arbitrary",), vmem_limit_bytes=VMEM_LIMIT_TABLE),
        name="peer_v",
    )(idx, act, expand, tab)


def _ple_kernel(x1_ref, y_ref, p_ref, g_ref, wg_ref, wp_ref, o_ref):
    y = jnp.concatenate([y_ref[:, r, :] for r in range(ROWS_PER_EXPERT)], axis=1)
    x2 = x1_ref[...] + y
    hg = _rms(x2, g_ref[...])
    gate = jax.nn.sigmoid(_dot(hg.astype(BF16), wg_ref[...]))
    o_ref[...] = x2 + gate * _dot(p_ref[...].astype(BF16), wp_ref[...])


def _ple(x1, y, p2d, g, wg, wp, tm):
    t, d = x1.shape
    row = lambda w: pl.BlockSpec((tm, w), lambda i: (i, 0))
    full = lambda a: pl.BlockSpec(a.shape, lambda i: (0,) * a.ndim)
    return pl.pallas_call(
        _ple_kernel,
        grid=(t // tm,),
        in_specs=[row(d), pl.BlockSpec((tm, ROWS_PER_EXPERT, LANES), lambda i: (i, 0, 0)), row(p2d.shape[1]),
                  full(g), full(wg), full(wp)],
        out_specs=row(d),
        out_shape=jax.ShapeDtypeStruct((t, d), F32),
        compiler_params=pltpu.CompilerParams(
            dimension_semantics=("parallel",), vmem_limit_bytes=VMEM_LIMIT_DENSE),
        name="ple",
    )(x1, y, p2d, g, wg, wp)


def _expert_table_kernel(emb_ref, tab_ref):
    ne = emb_ref.shape[0]

    def bf16_bits(x):
        return lax.bitcast_convert_type(x.astype(BF16).astype(F32), jnp.uint32)

    for s in range(EXPERT_WORDS):
        lo = bf16_bits(emb_ref[:, (2 * s) * LANES:(2 * s + 1) * LANES])
        hi = bf16_bits(emb_ref[:, (2 * s + 1) * LANES:(2 * s + 2) * LANES])
        tab_ref[pl.ds(s, ne, stride=EXPERT_WORDS), :] = (lo >> 16) | hi


def _expert_table(emb):
    n, d = emb.shape
    assert d == ROWS_PER_EXPERT * LANES
    ne = 256
    return pl.pallas_call(
        _expert_table_kernel,
        grid=(n // ne,),
        in_specs=[pl.BlockSpec((ne, d), lambda i: (i, 0))],
        out_specs=pl.BlockSpec((ne * EXPERT_WORDS, LANES), lambda i: (i, 0)),
        out_shape=jax.ShapeDtypeStruct((n * EXPERT_WORDS, LANES), jnp.uint32),
        compiler_params=pltpu.CompilerParams(dimension_semantics=("parallel",)),
        name="expert_table",
    )(emb)


def _layer(x, p, g_mix, w_in, q_norm, k_norm, sinks, w_out, g_ffn, w_query, sub_keys, emb_u, emb_v,
           g_ple, w_gate, w_proj):
    b, s, d = x.shape
    t = b * s
    tm = 256 if t % 256 == 0 else CHUNK
    tb = PEER_TOKEN_BLOCK
    assert t % tb == 0
    x2d = x.reshape(t, d)
    proj = _in_proj(x2d, g_mix[None, :], w_in.astype(BF16), tm)
    proj3 = proj.reshape(b, s, proj.shape[1])
    ro = _retention(proj3, _retention_tables(s))
    so = _swa(proj3, sinks, q_norm[None, :], k_norm[None, :])
    wo = w_out.astype(BF16)
    x1, h3, qp = _out_proj(x2d, ro.reshape(t, RET_WIDTH), so.reshape(t, SWA_Q_WIDTH),
                           wo[:RET_WIDTH], wo[RET_WIDTH:], g_ffn[None, :], w_query.astype(BF16), tm)
    idx, gate = _peer_topk(qp, sub_keys.astype(BF16), tm)
    act = _peer_u(idx, h3, gate, _expert_table(emb_u), tb)
    y = _peer_v(idx, act, _expert_table(emb_v), tb)
    out = _ple(x1, y, p.reshape(t, p.shape[-1]), g_ple[None, :],
               w_gate.astype(BF16), w_proj.astype(BF16), tm)
    return out.reshape(b, s, d)


def kernel(x, p, g_mix, w_in, q_norm, k_norm, sinks, w_out, g_ffn, peer_w_query, peer_sub_keys,
           peer_u, peer_v, g_ple, w_ple_gate, w_ple_proj):
    for i in range(p.shape[0]):
        x = _layer(x, p[i], g_mix[i], w_in[i], q_norm[i], k_norm[i], sinks[i], w_out[i], g_ffn[i],
                   peer_w_query[i], peer_sub_keys[i], peer_u[i], peer_v[i], g_ple[i], w_ple_gate[i],
                   w_ple_proj[i])
    return x
```

```python
import functools

import jax
import jax.numpy as jnp
from jax import lax
from jax.experimental import pallas as pl
from jax.experimental.pallas import tpu as pltpu

F32 = jnp.float32
BF16 = jnp.bfloat16

HEAD_DIM = 64
RET_HEADS = 8
SWA_Q_HEADS = 8
SWA_KV_HEADS = 2
SWA_GROUP = SWA_Q_HEADS // SWA_KV_HEADS
RET_WIDTH = RET_HEADS * HEAD_DIM
SWA_Q_WIDTH = SWA_Q_HEADS * HEAD_DIM
SWA_KV_WIDTH = SWA_KV_HEADS * HEAD_DIM
CHUNK = 128
ROPE_BASE = 10000.0
PEER_HEADS = 8
PEER_KEYS = 128
PEER_HALF = 128
PEER_TOPK = 16
PEER_PAIRS = PEER_HEADS * PEER_TOPK
EPS = 1e-6

LANES = 128
F32_SUBLANES = 8
BF16_SUBLANES = 16
VMEM_LIMIT_DENSE = 48 * 1024 * 1024
VMEM_LIMIT_TABLE = 56 * 1024 * 1024

ROWS_PER_EXPERT = 1024 // LANES
TILE_ROWS = BF16_SUBLANES


def _rms(x, g):
    ms = jnp.mean(x * x, axis=-1, keepdims=True)
    return x * lax.rsqrt(ms + EPS) * g


def _dot(a, b):
    return jnp.dot(a, b, preferred_element_type=F32)


def _dot_nt(a, b):
    return lax.dot_general(a, b, (((1,), (1,)), ((), ())), preferred_element_type=F32)


def _dot_tn(a, b):
    return lax.dot_general(a, b, (((0,), (0,)), ((), ())), preferred_element_type=F32)


def _in_proj_kernel(x_ref, g_ref, w_ref, o_ref):
    h = _rms(x_ref[...], g_ref[...])
    o_ref[...] = _dot(h.astype(BF16), w_ref[...])


def _in_proj(x2d, g, w, tm):
    t, d = x2d.shape
    n = w.shape[1]
    return pl.pallas_call(
        _in_proj_kernel,
        grid=(t // tm,),
        in_specs=[
            pl.BlockSpec((tm, d), lambda i: (i, 0)),
            pl.BlockSpec((1, d), lambda i: (0, 0)),
            pl.BlockSpec((d, n), lambda i: (0, 0)),
        ],
        out_specs=pl.BlockSpec((tm, n), lambda i: (i, 0)),
        out_shape=jax.ShapeDtypeStruct((t, n), F32),
        compiler_params=pltpu.CompilerParams(
            dimension_semantics=("parallel",), vmem_limit_bytes=VMEM_LIMIT_DENSE),
        name="in_proj",
    )(x2d, g, w)


def _retention_kernel(cd_ref, q_ref, k_ref, v_ref, g_ref, cos_ref, sin_ref, dec_ref, wk_ref, wq_ref,
                      o_ref, state_ref):
    n = pl.program_id(1)

    @pl.when(n == 0)
    def _():
        state_ref[...] = jnp.zeros_like(state_ref)

    c = cos_ref[...]
    s = sin_ref[...]
    lane = lax.broadcasted_iota(jnp.int32, (CHUNK, RET_WIDTH), 1)
    first_half = (lane % HEAD_DIM) < (HEAD_DIM // 2)

    def rot(x):
        partner = jnp.where(first_half,
                            pltpu.roll(x, RET_WIDTH - HEAD_DIM // 2, 1),
                            pltpu.roll(x, HEAD_DIM // 2, 1))
        return x * c + partner * s

    wq = wq_ref[...]
    for bb in range(q_ref.shape[0]):
        q = rot(q_ref[bb])
        k = rot(k_ref[bb]) * (HEAD_DIM ** -0.5)
        kw = k * wk_ref[...]
        v = v_ref[bb]
        gate = g_ref[bb]
        for h in range(RET_HEADS):
            sl = slice(h * HEAD_DIM, (h + 1) * HEAD_DIM)
            qh = q[:, sl].astype(BF16)
            kh = k[:, sl].astype(BF16)
            vh = v[:, sl].astype(BF16)
            att = _dot_nt(qh, kh) * dec_ref[h]
            o = _dot(att.astype(BF16), vh)
            st = state_ref[bb, h]
            o = o + _dot(qh, st.astype(BF16)) * wq[:, sl]
            state_ref[bb, h] = st * cd_ref[h] + _dot_tn(kw[:, sl].astype(BF16), vh)
            mu = jnp.mean(o, axis=-1, keepdims=True)
            oc = o - mu
            var = jnp.mean(oc * oc, axis=-1, keepdims=True)
            on = oc * lax.rsqrt(var + EPS)
            gh = gate[:, sl]
            o_ref[bb, :, sl] = gh * jax.nn.sigmoid(gh) * on


def _retention(proj3, tables):
    b, s, _ = proj3.shape
    cd, cos_t, sin_t, dec, wk, wq = tables
    nb = 1
    col = lambda j: pl.BlockSpec((nb, CHUNK, RET_WIDTH), lambda bi, ni, j=j: (bi, ni, j))
    pos_spec = pl.BlockSpec((CHUNK, RET_WIDTH), lambda bi, ni: (ni, 0))
    const2 = pl.BlockSpec((CHUNK, RET_WIDTH), lambda bi, ni: (0, 0))
    return pl.pallas_call(
        _retention_kernel,
        grid=(b // nb, s // CHUNK),
        in_specs=[
            pl.BlockSpec(memory_space=pltpu.SMEM),
            col(0), col(1), col(2), col(3),
            pos_spec, pos_spec,
            pl.BlockSpec((RET_HEADS, CHUNK, CHUNK), lambda bi, ni: (0, 0, 0)),
            const2, const2,
        ],
        out_specs=pl.BlockSpec((nb, CHUNK, RET_WIDTH), lambda bi, ni: (bi, ni, 0)),
        out_shape=jax.ShapeDtypeStruct((b, s, RET_WIDTH), F32),
        scratch_shapes=[pltpu.VMEM((nb, RET_HEADS, HEAD_DIM, HEAD_DIM), F32)],
        compiler_params=pltpu.CompilerParams(
            dimension_semantics=("parallel", "arbitrary"), vmem_limit_bytes=VMEM_LIMIT_DENSE),
        name="retention",
    )(cd, proj3, proj3, proj3, proj3, cos_t, sin_t, dec, wk, wq)


def _retention_tables(s):
    half = HEAD_DIM // 2
    pos = jnp.arange(s, dtype=F32)
    freqs = ROPE_BASE ** (-jnp.arange(half, dtype=F32) / half)
    ang = pos[:, None] * freqs[None, :]
    c = jnp.cos(ang)
    sn = jnp.sin(ang)
    cos_t = jnp.tile(jnp.concatenate([c, c], axis=-1), (1, RET_HEADS))
    sin_t = jnp.tile(jnp.concatenate([-sn, sn], axis=-1), (1, RET_HEADS))
    gamma = 1.0 - jnp.exp2(-5.0 - jnp.arange(RET_HEADS, dtype=F32))
    log_g = jnp.log(gamma)
    idx = jnp.arange(CHUNK)
    rel = idx[:, None] - idx[None, :]
    dec = jnp.where(rel[None] >= 0,
                    jnp.exp(log_g[:, None, None] * jnp.maximum(rel, 0)[None].astype(F32)),
                    0.0)
    w_k = jnp.exp(log_g[None, :] * (CHUNK - 1 - idx)[:, None].astype(F32))
    w_q = jnp.exp(log_g[None, :] * (idx + 1)[:, None].astype(F32))
    cd = jnp.exp(log_g * CHUNK)
    wk = jnp.repeat(w_k, HEAD_DIM, axis=1)
    wq = jnp.repeat(w_q, HEAD_DIM, axis=1)
    return cd, cos_t, sin_t, dec, wk, wq


def _swa_kernel(sink_ref, q_ref, kc_ref, vc_ref, kp_ref, vp_ref, qn_ref, kn_ref, o_ref):
    n = pl.program_id(1)
    rows = SWA_GROUP * CHUNK
    row = lax.broadcasted_iota(jnp.int32, (rows, 2 * CHUNK), 0)
    ki = lax.broadcasted_iota(jnp.int32, (rows, 2 * CHUNK), 1)
    rel = row % CHUNK + CHUNK - ki
    mask = (rel >= 0) & (rel < CHUNK) & ((n > 0) | (ki >= CHUNK))
    row_head = lax.broadcasted_iota(jnp.int32, (rows, 1), 0) // CHUNK
    qn = qn_ref[...]
    kn = kn_ref[...]
    for bb in range(q_ref.shape[0]):
        q = q_ref[bb]
        for g in range(SWA_KV_HEADS):
            sl = slice(g * HEAD_DIM, (g + 1) * HEAD_DIM)
            kcat = jnp.concatenate([kp_ref[bb, :, sl], kc_ref[bb, :, sl]], axis=0)
            kcat = _rms(kcat, kn).astype(BF16)
            vcat = jnp.concatenate([vp_ref[bb, :, sl], vc_ref[bb, :, sl]], axis=0).astype(BF16)
            heads = [g * SWA_GROUP + j for j in range(SWA_GROUP)]
            qs = jnp.concatenate([_rms(q[:, h * HEAD_DIM:(h + 1) * HEAD_DIM], qn) for h in heads], axis=0)
            sink = jnp.full((rows, 1), sink_ref[heads[0]], F32)
            for j in range(1, SWA_GROUP):
                sink = jnp.where(row_head == j, sink_ref[heads[j]], sink)
            sc = _dot_nt(qs.astype(BF16), kcat) * (HEAD_DIM ** -0.5)
            sc = jnp.where(mask, sc, -jnp.inf)
            m = jnp.maximum(jnp.max(sc, axis=-1, keepdims=True), sink)
            e = jnp.exp(sc - m)
            pr = e / (jnp.sum(e, axis=-1, keepdims=True) + jnp.exp(sink - m))
            o = _dot(pr.astype(BF16), vcat)
            for j, h in enumerate(heads):
                o_ref[bb, :, h * HEAD_DIM:(h + 1) * HEAD_DIM] = o[j * CHUNK:(j + 1) * CHUNK]


def _swa(proj3, sinks, qn, kn):
    b, s, _ = proj3.shape
    q_blk = (4 * RET_WIDTH) // SWA_Q_WIDTH
    k_blk = (4 * RET_WIDTH + SWA_Q_WIDTH) // SWA_KV_WIDTH
    v_blk = k_blk + 1
    nb = 1
    cur = lambda j: pl.BlockSpec((nb, CHUNK, SWA_KV_WIDTH), lambda bi, ni, j=j: (bi, ni, j))
    prev = lambda j: pl.BlockSpec((nb, CHUNK, SWA_KV_WIDTH),
                                  lambda bi, ni, j=j: (bi, jnp.maximum(ni - 1, 0), j))
    return pl.pallas_call(
        _swa_kernel,
        grid=(b // nb, s // CHUNK),
        in_specs=[
            pl.BlockSpec(memory_space=pltpu.SMEM),
            pl.BlockSpec((nb, CHUNK, SWA_Q_WIDTH), lambda bi, ni: (bi, ni, q_blk)),
            cur(k_blk), cur(v_blk), prev(k_blk), prev(v_blk),
            pl.BlockSpec((1, HEAD_DIM), lambda bi, ni: (0, 0)),
            pl.BlockSpec((1, HEAD_DIM), lambda bi, ni: (0, 0)),
        ],
        out_specs=pl.BlockSpec((nb, CHUNK, SWA_Q_WIDTH), lambda bi, ni: (bi, ni, 0)),
        out_shape=jax.ShapeDtypeStruct((b, s, SWA_Q_WIDTH), F32),
        compiler_params=pltpu.CompilerParams(
            dimension_semantics=("parallel", "parallel"), vmem_limit_bytes=VMEM_LIMIT_DENSE),
        name="swa",
    )(sinks, proj3, proj3, proj3, proj3, proj3, qn, kn)


def _out_proj_kernel(x_ref, ro_ref, so_ref, wo1_ref, wo2_ref, g_ref, wq_ref, x1_ref, h2_ref, qp_ref):
    x1 = (x_ref[...] + _dot(ro_ref[...].astype(BF16), wo1_ref[...])
          + _dot(so_ref[...].astype(BF16), wo2_ref[...]))
    x1_ref[...] = x1
    h2 = _rms(x1, g_ref[...])
    for r in range(ROWS_PER_EXPERT):
        h2_ref[:, r, :] = h2[:, r * LANES:(r + 1) * LANES]
    qp = _dot(h2.astype(BF16), wq_ref[...])
    width = qp_ref.shape[2]
    for h in range(PEER_HEADS):
        qp_ref[h] = qp[:, h * width:(h + 1) * width]


def _out_proj(x2d, ro, so, wo1, wo2, g, wq, tm):
    t, d = x2d.shape
    nq = wq.shape[1]
    width = nq // PEER_HEADS
    row = lambda w: pl.BlockSpec((tm, w), lambda i: (i, 0))
    full = lambda a: pl.BlockSpec(a.shape, lambda i: (0,) * a.ndim)
    return pl.pallas_call(
        _out_proj_kernel,
        grid=(t // tm,),
        in_specs=[row(d), row(ro.shape[1]), row(so.shape[1]), full(wo1), full(wo2), full(g), full(wq)],
        out_specs=[row(d), pl.BlockSpec((tm, ROWS_PER_EXPERT, LANES), lambda i: (i, 0, 0)),
                   pl.BlockSpec((PEER_HEADS, tm, width), lambda i: (0, i, 0))],
        out_shape=[jax.ShapeDtypeStruct((t, d), F32), jax.ShapeDtypeStruct((t, ROWS_PER_EXPERT, LANES), F32),
                   jax.ShapeDtypeStruct((PEER_HEADS, t, width), F32)],
        compiler_params=pltpu.CompilerParams(
            dimension_semantics=("parallel",), vmem_limit_bytes=VMEM_LIMIT_DENSE),
        name="out_proj",
    )(x2d, ro, so, wo1, wo2, g, wq)


_CAND_B = (16, 8, 8, 8, 8, 8, 8, 8)
TOPK_HEADS_PER_STEP = 8


def _candidate_flat_index():
    pos = [a * PEER_TOPK + b for a, nb in enumerate(_CAND_B) for b in range(nb)]
    pos += [a * PEER_TOPK for a in range(len(_CAND_B), PEER_TOPK)]
    return pos


def _peer_topk_kernel(qp_ref, keys_ref, pos_ref, idx_ref, gate_ref,
                      s_ref, i_ref, best_ref, exp_ref, et_ref, gt_ref):
    g = pl.program_id(1)
    heads, tm = qp_ref.shape[0], qp_ref.shape[1]
    neg = -jnp.inf
    key_id = lax.broadcasted_iota(jnp.int32, (PEER_KEYS, tm), 0).astype(F32)
    pos = pos_ref[...]
    for hh in range(heads):
        for p in range(2):
            qhp = qp_ref[hh, :, p * PEER_HALF:(p + 1) * PEER_HALF].astype(BF16)
            sc = _dot_nt(keys_ref[p], qhp)
            for a in range(PEER_TOPK):
                m = jnp.max(sc, axis=0, keepdims=True)
                i = jnp.min(jnp.where(sc == m, key_id, float(PEER_KEYS)), axis=0, keepdims=True)
                sc = jnp.where(key_id == i, neg, sc)
                s_ref[hh, p, a:a + 1, :] = m
                i_ref[hh, p, a:a + 1, :] = i
    for hh in range(heads):
        s0, s1 = s_ref[hh, 0], s_ref[hh, 1]
        i0, i1 = i_ref[hh, 0] * float(PEER_KEYS), i_ref[hh, 1]
        cs_blocks, ci_blocks = [], []
        for a, nb in enumerate(_CAND_B):
            cs_blocks.append(s0[a:a + 1] + s1[0:nb])
            ci_blocks.append(i0[a:a + 1] + i1[0:nb])
        na = len(_CAND_B)
        cs_blocks.append(s0[na:] + s1[0:1])
        ci_blocks.append(i0[na:] + i1[0:1])
        cs = jnp.concatenate(cs_blocks, axis=0)
        ci = jnp.concatenate(ci_blocks, axis=0)
        for k in range(PEER_TOPK):
            m = jnp.max(cs, axis=0, keepdims=True)
            first = jnp.min(jnp.where(cs == m, pos, float(PEER_TOPK * PEER_TOPK)), axis=0, keepdims=True)
            sel = pos == first
            best_ref[hh, k:k + 1, :] = m
            exp_ref[hh, k:k + 1, :] = jnp.max(jnp.where(sel, ci, -1.0), axis=0, keepdims=True)
            cs = jnp.where(sel, neg, cs)
        best = best_ref[hh]
        ex = jnp.exp(best - best[0:1])
        gate = ex / jnp.sum(ex, axis=0, keepdims=True)
        rows = pl.ds(pl.multiple_of((g * heads + hh) * PEER_TOPK, PEER_TOPK), PEER_TOPK)
        et_ref[rows, :] = exp_ref[hh]
        gt_ref[rows, :] = gate

    @pl.when(g == pl.num_programs(1) - 1)
    def _():
        idx_ref[...] = et_ref[...].T.astype(jnp.int32) * EXPERT_WORDS
        gate_ref[...] = gt_ref[...].T


def _peer_topk(qp, keys, tm):
    nh, t, width = qp.shape
    hps = TOPK_HEADS_PER_STEP
    pos = _candidate_flat_index()
    pos = jnp.broadcast_to(jnp.asarray(pos, F32)[:, None], (len(pos), tm))
    full = lambda a: pl.BlockSpec(a.shape, lambda i, h: (0,) * a.ndim)
    out_spec = pl.BlockSpec((tm, PEER_PAIRS), lambda i, h: (i, 0))
    return pl.pallas_call(
        _peer_topk_kernel,
        grid=(t // tm, nh // hps),
        in_specs=[pl.BlockSpec((hps, tm, width), lambda i, h: (h, i, 0)), full(keys), full(pos)],
        out_specs=[out_spec, out_spec],
        out_shape=[jax.ShapeDtypeStruct((t, PEER_PAIRS), jnp.int32),
                   jax.ShapeDtypeStruct((t, PEER_PAIRS), F32)],
        scratch_shapes=[pltpu.VMEM((hps, 2, PEER_TOPK, tm), F32), pltpu.VMEM((hps, 2, PEER_TOPK, tm), F32),
                        pltpu.VMEM((hps, PEER_TOPK, tm), F32), pltpu.VMEM((hps, PEER_TOPK, tm), F32),
                        pltpu.VMEM((PEER_PAIRS, tm), F32), pltpu.VMEM((PEER_PAIRS, tm), F32)],
        compiler_params=pltpu.CompilerParams(
            dimension_semantics=("parallel", "arbitrary"), vmem_limit_bytes=VMEM_LIMIT_DENSE),
        name="peer_topk",
    )(qp, keys, pos)


EXPERT_WORDS = ROWS_PER_EXPERT // 2
SLOTS = PEER_PAIRS // 4
SLOT_ROWS = SLOTS * TILE_ROWS
MXU_ROWS = 256
SLOTS_PER_MXU_TILE = MXU_ROWS // TILE_ROWS
INDEX_SLOT_TOKENS = 64
PEER_TOKEN_BLOCK = 2 * INDEX_SLOT_TOKENS


class _IndexSlots:
    def __init__(self, idx_hbm, idx_ref, sem_ref):
        self.idx_hbm, self.idx_ref, self.sem_ref = idx_hbm, idx_ref, sem_ref
        self.step = pl.program_id(0)
        self.last = pl.num_programs(0) - 1

    def _copy(self, step, slot):
        row0 = pl.multiple_of((2 * step + slot) * INDEX_SLOT_TOKENS, INDEX_SLOT_TOKENS)
        return pltpu.make_async_copy(self.idx_hbm.at[pl.ds(row0, INDEX_SLOT_TOKENS)],
                                     self.idx_ref.at[slot], self.sem_ref.at[slot])

    def prime(self):
        @pl.when(self.step == 0)
        def _():
            self._copy(self.step, 0).start()
            self._copy(self.step, 1).start()

    def before_gather(self, t):
        if t % INDEX_SLOT_TOKENS == 0:
            self._copy(self.step, t // INDEX_SLOT_TOKENS).wait()

    def after_gather(self, t):
        if t % INDEX_SLOT_TOKENS == INDEX_SLOT_TOKENS - 1:
            self._copy(jnp.minimum(self.step + 1, self.last), t // INDEX_SLOT_TOKENS).start()

    def drain(self):
        @pl.when(self.step == self.last)
        def _():
            self._copy(self.step, 0).wait()
            self._copy(self.step, 1).wait()

    def __call__(self, t, j):
        return self.idx_ref[t // INDEX_SLOT_TOKENS, t % INDEX_SLOT_TOKENS, j]


def _index_scratch():
    return [pltpu.SMEM((2, INDEX_SLOT_TOKENS, PEER_PAIRS), jnp.int32), pltpu.SemaphoreType.DMA((2,))]


def _gather_slots(index, tab_ref, t, scale=None):
    index.before_gather(t)

    def expert(j):
        row0 = pl.multiple_of(index(t, j), EXPERT_WORDS)
        return tab_ref[pl.ds(row0, EXPERT_WORDS), :]

    def tile(j):
        words = jnp.concatenate([expert(j), expert(j + 1)], axis=0)
        x = pltpu.bitcast(words, BF16)
        return x if scale is None else x * scale

    blocks = []
    for k in range(SLOTS // SLOTS_PER_MXU_TILE):
        slots = [jnp.concatenate([tile(2 * i), tile(PEER_PAIRS // 2 + 2 * i)], axis=1)
                 for i in range(k * SLOTS_PER_MXU_TILE, (k + 1) * SLOTS_PER_MXU_TILE)]
        blocks.append(jnp.concatenate(slots, axis=0))
    index.after_gather(t)
    return blocks


def _peer_u_kernel(idx_hbm, h_ref, gate_ref, half_ref, tab_ref, act_ref, idx_ref, sem_ref):
    tb = h_ref.shape[0]
    ones = jnp.ones((BF16_SUBLANES, LANES), BF16)
    half_sum = half_ref[...]
    index = _IndexSlots(idx_hbm, idx_ref, sem_ref)
    index.prime()

    def gather(t):
        ht = h_ref[t].astype(BF16)
        return _gather_slots(index, tab_ref, t, scale=jnp.concatenate([ht, ht], axis=0))

    def half_sums(prod):
        z = [_dot(half_sum, block) for block in prod]
        return jnp.concatenate(z, axis=0).astype(BF16)

    def lane_sums(t, z):
        s = _dot_nt(ones, jnp.concatenate([z[:, :LANES], z[:, LANES:]], axis=0))
        act_ref[t:t + 1, :] = s[0:1, :]

    prod = gather(0)
    pending = None
    for t in range(tb):
        nxt = gather(t + 1) if t + 1 < tb else None
        z = half_sums(prod)
        if pending is not None:
            lane_sums(t - 1, pending)
        pending, prod = z, nxt
    lane_sums(tb - 1, pending)
    index.drain()
    a = act_ref[...]
    act_ref[...] = 0.5 * a * (1.0 + lax.erf(a * (2.0 ** -0.5))) * gate_ref[...]


def _peer_u(idx, h3, gate, tab, tb):
    t = idx.shape[0]
    r = jnp.arange(2 * SLOTS_PER_MXU_TILE)[:, None]
    c = jnp.arange(MXU_ROWS)[None, :]
    half_sum = ((c // TILE_ROWS == r // 2) & ((c % TILE_ROWS) // ROWS_PER_EXPERT == r % 2)).astype(BF16)
    row = pl.BlockSpec((tb, PEER_PAIRS), lambda i: (i, 0))
    return pl.pallas_call(
        _peer_u_kernel,
        grid=(t // tb,),
        in_specs=[
            pl.BlockSpec(memory_space=pl.ANY),
            pl.BlockSpec((tb, ROWS_PER_EXPERT, LANES), lambda i: (i, 0, 0)),
            row,
            pl.BlockSpec(half_sum.shape, lambda i: (0, 0)),
            pl.BlockSpec(memory_space=pltpu.VMEM),
        ],
        out_specs=row,
        out_shape=jax.ShapeDtypeStruct((t, PEER_PAIRS), F32),
        scratch_shapes=_index_scratch(),
        compiler_params=pltpu.CompilerParams(
            dimension_semantics=("arbitrary",), vmem_limit_bytes=VMEM_LIMIT_TABLE),
        name="peer_u",
    )(idx, h3, gate, half_sum, tab)


def _peer_v_kernel(idx_hbm, act_ref, exp_ref, tab_ref, y_ref, coef_ref, idx_ref, sem_ref):
    tb = act_ref.shape[0]
    n = ROWS_PER_EXPERT
    index = _IndexSlots(idx_hbm, idx_ref, sem_ref)
    index.prime()
    act = act_ref[...]
    hi = act.astype(BF16)
    lhs = jnp.concatenate([hi, (act - hi.astype(F32)).astype(BF16)], axis=0)
    lhs = jnp.broadcast_to(lhs[:, None, :], (2 * tb, n, PEER_PAIRS)).reshape(2 * tb * n, PEER_PAIRS)
    coef_ref[...] = _dot(lhs, exp_ref[...])
    shape = (n, SLOT_ROWS)
    own_row = (lax.broadcasted_iota(jnp.int32, shape, 1) % n) == lax.broadcasted_iota(jnp.int32, shape, 0)

    def consume(t, xs):
        def part(kind, side):
            r0 = (kind * tb + t) * n
            return jnp.where(own_row, coef_ref[r0:r0 + n, side * SLOT_ROWS:(side + 1) * SLOT_ROWS], 0.0)

        coef = jnp.concatenate([part(0, 0), part(0, 1), part(1, 0), part(1, 1)], axis=0).astype(BF16)
        res = _dot(coef, jnp.concatenate(xs, axis=0))
        y_ref[t] =((res[0:n, :LANES] + res[2 * n:3 * n, :LANES])
                    + (res[n:2 * n, LANES:] + res[3 * n:4 * n, LANES:]))

    xs = _gather_slots(index, tab_ref, 0)
    for t in range(tb):
        nxt = _gather_slots(index, tab_ref, t + 1) if t + 1 < tb else None
        consume(t, xs)
        xs = nxt
    index.drain()


def _peer_v(idx, act, tab, tb):
    t = idx.shape[0]
    j = jnp.arange(PEER_PAIRS)[:, None]
    c = jnp.arange(2 * SLOT_ROWS)[None, :]
    pair_of_col = ((PEER_PAIRS // 2) * (c // SLOT_ROWS) + 2 * ((c % SLOT_ROWS) // TILE_ROWS)
                   + (c % TILE_ROWS) // ROWS_PER_EXPERT)
    expand = (pair_of_col == j).astype(BF16)
    row = pl.BlockSpec((tb, PEER_PAIRS), lambda i: (i, 0))
    return pl.pallas_call(
        _peer_v_kernel,
        grid=(t // tb,),
        in_specs=[
            pl.BlockSpec(memory_space=pl.ANY),
            row,
            pl.BlockSpec(expand.shape, lambda i: (0, 0)),
            pl.BlockSpec(memory_space=pltpu.VMEM),
        ],
        out_specs=pl.BlockSpec((tb, ROWS_PER_EXPERT, LANES), lambda i: (i, 0, 0)),
        out_shape=jax.ShapeDtypeStruct((t, ROWS_PER_EXPERT, LANES), F32),
        scratch_shapes=[pltpu.VMEM((2 * tb * ROWS_PER_EXPERT, 2 * SLOT_ROWS), F32)] + _index_scratch(),
        compiler_params=pltpu.CompilerParams(
            dimension_semantics=("arbitrary",), vmem_limit_bytes=VMEM_LIMIT_TABLE),
        name="peer_v",
    )(idx, act, expand, tab)


def _ple_kernel(x1_ref, y_ref, p_ref, g_ref, wg_ref, wp_ref, o_ref):
    y = jnp.concatenate([y_ref[:, r, :] for r in range(ROWS_PER_EXPERT)], axis=1)
    x2 = x1_ref[...] + y
    hg = _rms(x2, g_ref[...])
    gate = jax.nn.sigmoid(_dot(hg.astype(BF16), wg_ref[...]))
    o_ref[...] = x2 + gate * _dot(p_ref[...].astype(BF16), wp_ref[...])


def _ple(x1, y, p2d, g, wg, wp, tm):
    t, d = x1.shape
    row = lambda w: pl.BlockSpec((tm, w), lambda i: (i, 0))
    full = lambda a: pl.BlockSpec(a.shape, lambda i: (0,) * a.ndim)
    return pl.pallas_call(
        _ple_kernel,
        grid=(t // tm,),
        in_specs=[row(d), pl.BlockSpec((tm, ROWS_PER_EXPERT, LANES), lambda i: (i, 0, 0)), row(p2d.shape[1]),
                  full(g), full(wg), full(wp)],
        out_specs=row(d),
        out_shape=jax.ShapeDtypeStruct((t, d), F32),
        compiler_params=pltpu.CompilerParams(
            dimension_semantics=("parallel",), vmem_limit_bytes=VMEM_LIMIT_DENSE),
        name="ple",
    )(x1, y, p2d, g, wg, wp)


def _expert_table_kernel(emb_ref, tab_ref):
    ne = emb_ref.shape[0]

    def bf16_bits(x):
        return lax.bitcast_convert_type(x.astype(BF16).astype(F32), jnp.uint32)

    for s in range(EXPERT_WORDS):
        lo = bf16_bits(emb_ref[:, (2 * s) * LANES:(2 * s + 1) * LANES])
        hi = bf16_bits(emb_ref[:, (2 * s + 1) * LANES:(2 * s + 2) * LANES])
        tab_ref[pl.ds(s, ne, stride=EXPERT_WORDS), :] = (lo >> 16) | hi


def _expert_table(emb):
    n, d = emb.shape
    assert d == ROWS_PER_EXPERT * LANES
    ne = 256
    return pl.pallas_call(
        _expert_table_kernel,
        grid=(n // ne,),
        in_specs=[pl.BlockSpec((ne, d), lambda i: (i, 0))],
        out_specs=pl.BlockSpec((ne * EXPERT_WORDS, LANES), lambda i: (i, 0)),
        out_shape=jax.ShapeDtypeStruct((n * EXPERT_WORDS, LANES), jnp.uint32),
        compiler_params=pltpu.CompilerParams(dimension_semantics=("parallel",)),
        name="expert_table",
    )(emb)


def _layer(x, p, g_mix, w_in, q_norm, k_norm, sinks, w_out, g_ffn, w_query, sub_keys, emb_u, emb_v,
           g_ple, w_gate, w_proj):
    b, s, d = x.shape
    t = b * s
    tm = 256 if t % 256 == 0 else CHUNK
    tb = PEER_TOKEN_BLOCK
    assert t % tb == 0
    x2d = x.reshape(t, d)
    proj = _in_proj(x2d, g_mix[None, :], w_in.astype(BF16), tm)
    proj3 = proj.reshape(b, s, proj.shape[1])
    ro = _retention(proj3, _retention_tables(s))
    so = _swa(proj3, sinks, q_norm[None, :], k_norm[None, :])
    wo = w_out.astype(BF16)
    x1, h3, qp = _out_proj(x2d, ro.reshape(t, RET_WIDTH), so.reshape(t, SWA_Q_WIDTH),
                           wo[:RET_WIDTH], wo[RET_WIDTH:], g_ffn[None, :], w_query.astype(BF16), tm)
    idx, gate = _peer_topk(qp, sub_keys.astype(BF16), tm)
    act = _peer_u(idx, h3, gate, _expert_table(emb_u), tb)
    y = _peer_v(idx, act, _expert_table(emb_v), tb)
    out = _ple(x1, y, p.reshape(t, p.shape[-1]), g_ple[None, :],
               w_gate.astype(BF16), w_proj.astype(BF16), tm)
    return out.reshape(b, s, d)


def kernel(x, p, g_mix, w_in, q_norm, k_norm, sinks, w_out, g_ffn, peer_w_query, peer_sub_keys,
           peer_u, peer_v, g_ple, w_ple_gate, w_ple_proj):
    for i in range(p.shape[0]):
        x = _layer(x, p[i], g_mix[i], w_in[i], q_norm[i], k_norm[i], sinks[i], w_out[i], g_ffn[i],
                   peer_w_query[i], peer_sub_keys[i], peer_u[i], peer_v[i], g_ple[i], w_ple_gate[i],
                   w_ple_proj[i])
    return x
```

```python
import functools

import jax
import jax.numpy as jnp
from jax import lax
from jax.experimental import pallas as pl
from jax.experimental.pallas import tpu as pltpu

F32 = jnp.float32
BF16 = jnp.bfloat16

HEAD_DIM = 64
RET_HEADS = 8
SWA_Q_HEADS = 8
SWA_KV_HEADS = 2
SWA_GROUP = SWA_Q_HEADS // SWA_KV_HEADS
RET_WIDTH = RET_HEADS * HEAD_DIM
SWA_Q_WIDTH = SWA_Q_HEADS * HEAD_DIM
SWA_KV_WIDTH = SWA_KV_HEADS * HEAD_DIM
CHUNK = 128
ROPE_BASE = 10000.0
PEER_HEADS = 8
PEER_KEYS = 128
PEER_HALF = 128
PEER_TOPK = 16
PEER_PAIRS = PEER_HEADS * PEER_TOPK
EPS = 1e-6

LANES = 128
F32_SUBLANES = 8
BF16_SUBLANES = 16
VMEM_LIMIT_DENSE = 48 * 1024 * 1024
VMEM_LIMIT_TABLE = 56 * 1024 * 1024

ROWS_PER_EXPERT = 1024 // LANES
TILE_ROWS = BF16_SUBLANES


def _rms(x, g):
    ms = jnp.mean(x * x, axis=-1, keepdims=True)
    return x * lax.rsqrt(ms + EPS) * g


def _dot(a, b):
    return jnp.dot(a, b, preferred_element_type=F32)


def _dot_nt(a, b):
    return lax.dot_general(a, b, (((1,), (1,)), ((), ())), preferred_element_type=F32)


def _dot_tn(a, b):
    return lax.dot_general(a, b, (((0,), (0,)), ((), ())), preferred_element_type=F32)


def _in_proj_kernel(x_ref, g_ref, w_ref, o_ref):
    h = _rms(x_ref[...], g_ref[...])
    o_ref[...] = _dot(h.astype(BF16), w_ref[...])


def _in_proj(x2d, g, w, tm):
    t, d = x2d.shape
    n = w.shape[1]
    return pl.pallas_call(
        _in_proj_kernel,
        grid=(t // tm,),
        in_specs=[
            pl.BlockSpec((tm, d), lambda i: (i, 0)),
            pl.BlockSpec((1, d), lambda i: (0, 0)),
            pl.BlockSpec((d, n), lambda i: (0, 0)),
        ],
        out_specs=pl.BlockSpec((tm, n), lambda i: (i, 0)),
        out_shape=jax.ShapeDtypeStruct((t, n), F32),
        compiler_params=pltpu.CompilerParams(
            dimension_semantics=("parallel",), vmem_limit_bytes=VMEM_LIMIT_DENSE),
        name="in_proj",
    )(x2d, g, w)


def _retention_kernel(cd_ref, q_ref, k_ref, v_ref, g_ref, cos_ref, sin_ref, dec_ref, wk_ref, wq_ref,
                      o_ref, state_ref):
    n = pl.program_id(1)

    @pl.when(n == 0)
    def _():
        state_ref[...] = jnp.zeros_like(state_ref)

    c = cos_ref[...]
    s = sin_ref[...]
    lane = lax.broadcasted_iota(jnp.int32, (CHUNK, RET_WIDTH), 1)
    first_half = (lane % HEAD_DIM) < (HEAD_DIM // 2)

    def rot(x):
        partner = jnp.where(first_half,
                            pltpu.roll(x, RET_WIDTH - HEAD_DIM // 2, 1),
                            pltpu.roll(x, HEAD_DIM // 2, 1))
        return x * c + partner * s

    wq = wq_ref[...]
    for bb in range(q_ref.shape[0]):
        q = rot(q_ref[bb])
        k = rot(k_ref[bb]) * (HEAD_DIM ** -0.5)
        kw = k * wk_ref[...]
        v = v_ref[bb]
        gate = g_ref[bb]
        for h in range(RET_HEADS):
            sl = slice(h * HEAD_DIM, (h + 1) * HEAD_DIM)
            qh = q[:, sl].astype(BF16)
            kh = k[:, sl].astype(BF16)
            vh = v[:, sl].astype(BF16)
            att = _dot_nt(qh, kh) * dec_ref[h]
            o = _dot(att.astype(BF16), vh)
            st = state_ref[bb, h]
            o = o + _dot(qh, st.astype(BF16)) * wq[:, sl]
            state_ref[bb, h] = st * cd_ref[h] + _dot_tn(kw[:, sl].astype(BF16), vh)
            mu = jnp.mean(o, axis=-1, keepdims=True)
            oc = o - mu
            var = jnp.mean(oc * oc, axis=-1, keepdims=True)
            on = oc * lax.rsqrt(var + EPS)
            gh = gate[:, sl]
            o_ref[bb, :, sl] = gh * jax.nn.sigmoid(gh) * on


def _retention(proj3, tables):
    b, s, _ = proj3.shape
    cd, cos_t, sin_t, dec, wk, wq = tables
    nb = 1
    col = lambda j: pl.BlockSpec((nb, CHUNK, RET_WIDTH), lambda bi, ni, j=j: (bi, ni, j))
    pos_spec = pl.BlockSpec((CHUNK, RET_WIDTH), lambda bi, ni: (ni, 0))
    const2 = pl.BlockSpec((CHUNK, RET_WIDTH), lambda bi, ni: (0, 0))
    return pl.pallas_call(
        _retention_kernel,
        grid=(b // nb, s // CHUNK),
        in_specs=[
            pl.BlockSpec(memory_space=pltpu.SMEM),
            col(0), col(1), col(2), col(3),
            pos_spec, pos_spec,
            pl.BlockSpec((RET_HEADS, CHUNK, CHUNK), lambda bi, ni: (0, 0, 0)),
            const2, const2,
        ],
        out_specs=pl.BlockSpec((nb, CHUNK, RET_WIDTH), lambda bi, ni: (bi, ni, 0)),
        out_shape=jax.ShapeDtypeStruct((b, s, RET_WIDTH), F32),
        scratch_shapes=[pltpu.VMEM((nb, RET_HEADS, HEAD_DIM, HEAD_DIM), F32)],
        compiler_params=pltpu.CompilerParams(
            dimension_semantics=("parallel", "arbitrary"), vmem_limit_bytes=VMEM_LIMIT_DENSE),
        name="retention",
    )(cd, proj3, proj3, proj3, proj3, cos_t, sin_t, dec, wk, wq)


def _retention_tables(s):
    half = HEAD_DIM // 2
    pos = jnp.arange(s, dtype=F32)
    freqs = ROPE_BASE ** (-jnp.arange(half, dtype=F32) / half)
    ang = pos[:, None] * freqs[None, :]
    c = jnp.cos(ang)
    sn = jnp.sin(ang)
    cos_t = jnp.tile(jnp.concatenate([c, c], axis=-1), (1, RET_HEADS))
    sin_t = jnp.tile(jnp.concatenate([-sn, sn], axis=-1), (1, RET_HEADS))
    gamma = 1.0 - jnp.exp2(-5.0 - jnp.arange(RET_HEADS, dtype=F32))
    log_g = jnp.log(gamma)
    idx = jnp.arange(CHUNK)
    rel = idx[:, None] - idx[None, :]
    dec = jnp.where(rel[None] >= 0,
                    jnp.exp(log_g[:, None, None] * jnp.maximum(rel, 0)[None].astype(F32)),
                    0.0)
    w_k = jnp.exp(log_g[None, :] * (CHUNK - 1 - idx)[:, None].astype(F32))
    w_q = jnp.exp(log_g[None, :] * (idx + 1)[:, None].astype(F32))
    cd = jnp.exp(log_g * CHUNK)
    wk = jnp.repeat(w_k, HEAD_DIM, axis=1)
    wq = jnp.repeat(w_q, HEAD_DIM, axis=1)
    return cd, cos_t, sin_t, dec, wk, wq


def _swa_kernel(sink_ref, q_ref, kc_ref, vc_ref, kp_ref, vp_ref, qn_ref, kn_ref, o_ref):
    n = pl.program_id(1)
    rows = SWA_GROUP * CHUNK
    row = lax.broadcasted_iota(jnp.int32, (rows, 2 * CHUNK), 0)
    ki = lax.broadcasted_iota(jnp.int32, (rows, 2 * CHUNK), 1)
    rel = row % CHUNK + CHUNK - ki
    mask = (rel >= 0) & (rel < CHUNK) & ((n > 0) | (ki >= CHUNK))
    row_head = lax.broadcasted_iota(jnp.int32, (rows, 1), 0) // CHUNK
    qn = qn_ref[...]
    kn = kn_ref[...]
    for bb in range(q_ref.shape[0]):
        q = q_ref[bb]
        for g in range(SWA_KV_HEADS):
            sl = slice(g * HEAD_DIM, (g + 1) * HEAD_DIM)
            kcat = jnp.concatenate([kp_ref[bb, :, sl], kc_ref[bb, :, sl]], axis=0)
            kcat = _rms(kcat, kn).astype(BF16)
            vcat = jnp.concatenate([vp_ref[bb, :, sl], vc_ref[bb, :, sl]], axis=0).astype(BF16)
            heads = [g * SWA_GROUP + j for j in range(SWA_GROUP)]
            qs = jnp.concatenate([_rms(q[:, h * HEAD_DIM:(h + 1) * HEAD_DIM], qn) for h in heads], axis=0)
            sink = jnp.full((rows, 1), sink_ref[heads[0]], F32)
            for j in range(1, SWA_GROUP):
                sink = jnp.where(row_head == j, sink_ref[heads[j]], sink)
            sc = _dot_nt(qs.astype(BF16), kcat) * (HEAD_DIM ** -0.5)
            sc = jnp.where(mask, sc, -jnp.inf)
            m = jnp.maximum(jnp.max(sc, axis=-1, keepdims=True), sink)
            e = jnp.exp(sc - m)
            pr = e / (jnp.sum(e, axis=-1, keepdims=True) + jnp.exp(sink - m))
            o = _dot(pr.astype(BF16), vcat)
            for j, h in enumerate(heads):
                o_ref[bb, :, h * HEAD_DIM:(h + 1) * HEAD_DIM] = o[j * CHUNK:(j + 1) * CHUNK]


def _swa(proj3, sinks, qn, kn):
    b, s, _ = proj3.shape
    q_blk = (4 * RET_WIDTH) // SWA_Q_WIDTH
    k_blk = (4 * RET_WIDTH + SWA_Q_WIDTH) // SWA_KV_WIDTH
    v_blk = k_blk + 1
    nb = 1
    cur = lambda j: pl.BlockSpec((nb, CHUNK, SWA_KV_WIDTH), lambda bi, ni, j=j: (bi, ni, j))
    prev = lambda j: pl.BlockSpec((nb, CHUNK, SWA_KV_WIDTH),
                                  lambda bi, ni, j=j: (bi, jnp.maximum(ni - 1, 0), j))
    return pl.pallas_call(
        _swa_kernel,
        grid=(b // nb, s // CHUNK),
        in_specs=[
            pl.BlockSpec(memory_space=pltpu.SMEM),
            pl.BlockSpec((nb, CHUNK, SWA_Q_WIDTH), lambda bi, ni: (bi, ni, q_blk)),
            cur(k_blk), cur(v_blk), prev(k_blk), prev(v_blk),
            pl.BlockSpec((1, HEAD_DIM), lambda bi, ni: (0, 0)),
            pl.BlockSpec((1, HEAD_DIM), lambda bi, ni: (0, 0)),
        ],
        out_specs=pl.BlockSpec((nb, CHUNK, SWA_Q_WIDTH), lambda bi, ni: (bi, ni, 0)),
        out_shape=jax.ShapeDtypeStruct((b, s, SWA_Q_WIDTH), F32),
        compiler_params=pltpu.CompilerParams(
            dimension_semantics=("parallel", "parallel"), vmem_limit_bytes=VMEM_LIMIT_DENSE),
        name="swa",
    )(sinks, proj3, proj3, proj3, proj3, proj3, qn, kn)


def _out_proj_kernel(x_ref, ro_ref, so_ref, wo1_ref, wo2_ref, g_ref, wq_ref, x1_ref, h2_ref, qp_ref):
    x1 = (x_ref[...] + _dot(ro_ref[...].astype(BF16), wo1_ref[...])
          + _dot(so_ref[...].astype(BF16), wo2_ref[...]))
    x1_ref[...] = x1
    h2 = _rms(x1, g_ref[...])
    for r in range(ROWS_PER_EXPERT):
        h2_ref[:, r, :] = h2[:, r * LANES:(r + 1) * LANES]
    qp = _dot(h2.astype(BF16), wq_ref[...])
    width = qp_ref.shape[2]
    for h in range(PEER_HEADS):
        qp_ref[h] = qp[:, h * width:(h + 1) * width]


def _out_proj(x2d, ro, so, wo1, wo2, g, wq, tm):
    t, d = x2d.shape
    nq = wq.shape[1]
    width = nq // PEER_HEADS
    row = lambda w: pl.BlockSpec((tm, w), lambda i: (i, 0))
    full = lambda a: pl.BlockSpec(a.shape, lambda i: (0,) * a.ndim)
    return pl.pallas_call(
        _out_proj_kernel,
        grid=(t // tm,),
        in_specs=[row(d), row(ro.shape[1]), row(so.shape[1]), full(wo1), full(wo2), full(g), full(wq)],
        out_specs=[row(d), pl.BlockSpec((tm, ROWS_PER_EXPERT, LANES), lambda i: (i, 0, 0)),
                   pl.BlockSpec((PEER_HEADS, tm, width), lambda i: (0, i, 0))],
        out_shape=[jax.ShapeDtypeStruct((t, d), F32), jax.ShapeDtypeStruct((t, ROWS_PER_EXPERT, LANES), F32),
                   jax.ShapeDtypeStruct((PEER_HEADS, t, width), F32)],
        compiler_params=pltpu.CompilerParams(
            dimension_semantics=("parallel",), vmem_limit_bytes=VMEM_LIMIT_DENSE),
        name="out_proj",
    )(x2d, ro, so, wo1, wo2, g, wq)


_CAND_B = (16, 8, 8, 8, 8, 8, 8, 8)
TOPK_HEADS_PER_STEP = 8


def _candidate_flat_index():
    pos = [a * PEER_TOPK + b for a, nb in enumerate(_CAND_B) for b in range(nb)]
    pos += [a * PEER_TOPK for a in range(len(_CAND_B), PEER_TOPK)]
    return pos


def _peer_topk_kernel(qp_ref, keys_ref, pos_ref, idx_ref, gate_ref,
                      s_ref, i_ref, best_ref, exp_ref, et_ref, gt_ref):
    g = pl.program_id(1)
    heads, tm = qp_ref.shape[0], qp_ref.shape[1]
    neg = -jnp.inf
    key_id = lax.broadcasted_iota(jnp.int32, (PEER_KEYS, tm), 0).astype(F32)
    pos = pos_ref[...]
    for hh in range(heads):
        for p in range(2):
            qhp = qp_ref[hh, :, p * PEER_HALF:(p + 1) * PEER_HALF].astype(BF16)
            sc = _dot_nt(keys_ref[p], qhp)
            for a in range(PEER_TOPK):
                m = jnp.max(sc, axis=0, keepdims=True)
                i = jnp.min(jnp.where(sc == m, key_id, float(PEER_KEYS)), axis=0, keepdims=True)
                sc = jnp.where(key_id == i, neg, sc)
                s_ref[hh, p, a:a + 1, :] = m
                i_ref[hh, p, a:a + 1, :] = i
    for hh in range(heads):
        s0, s1 = s_ref[hh, 0], s_ref[hh, 1]
        i0, i1 = i_ref[hh, 0] * float(PEER_KEYS), i_ref[hh, 1]
        cs_blocks, ci_blocks = [], []
        for a, nb in enumerate(_CAND_B):
            cs_blocks.append(s0[a:a + 1] + s1[0:nb])
            ci_blocks.append(i0[a:a + 1] + i1[0:nb])
        na = len(_CAND_B)
        cs_blocks.append(s0[na:] + s1[0:1])
        ci_blocks.append(i0[na:] + i1[0:1])
        cs = jnp.concatenate(cs_blocks, axis=0)
        ci = jnp.concatenate(ci_blocks, axis=0)
        for k in range(PEER_TOPK):
            m = jnp.max(cs, axis=0, keepdims=True)
            first = jnp.min(jnp.where(cs == m, pos, float(PEER_TOPK * PEER_TOPK)), axis=0, keepdims=True)
            sel = pos == first
            best_ref[hh, k:k + 1, :] = m
            exp_ref[hh, k:k + 1, :] = jnp.max(jnp.where(sel, ci, -1.0), axis=0, keepdims=True)
            cs = jnp.where(sel, neg, cs)
        best = best_ref[hh]
        ex = jnp.exp(best - best[0:1])
        gate = ex / jnp.sum(ex, axis=0, keepdims=True)
        rows = pl.ds(pl.multiple_of((g * heads + hh) * PEER_TOPK, PEER_TOPK), PEER_TOPK)
        et_ref[rows, :] = exp_ref[hh]
        gt_ref[rows, :] = gate

    @pl.when(g == pl.num_programs(1) - 1)
    def _():
        idx_ref[...] = et_ref[...].T.astype(jnp.int32) * EXPERT_WORDS
        gate_ref[...] = gt_ref[...].T


def _peer_topk(qp, keys, tm):
    nh, t, width = qp.shape
    hps = TOPK_HEADS_PER_STEP
    pos = _candidate_flat_index()
    pos = jnp.broadcast_to(jnp.asarray(pos, F32)[:, None], (len(pos), tm))
    full = lambda a: pl.BlockSpec(a.shape, lambda i, h: (0,) * a.ndim)
    out_spec = pl.BlockSpec((tm, PEER_PAIRS), lambda i, h: (i, 0))
    return pl.pallas_call(
        _peer_topk_kernel,
        grid=(t // tm, nh // hps),
        in_specs=[pl.BlockSpec((hps, tm, width), lambda i, h: (h, i, 0)), full(keys), full(pos)],
        out_specs=[out_spec, out_spec],
        out_shape=[jax.ShapeDtypeStruct((t, PEER_PAIRS), jnp.int32),
                   jax.ShapeDtypeStruct((t, PEER_PAIRS), F32)],
        scratch_shapes=[pltpu.VMEM((hps, 2, PEER_TOPK, tm), F32), pltpu.VMEM((hps, 2, PEER_TOPK, tm), F32),
                        pltpu.VMEM((hps, PEER_TOPK, tm), F32), pltpu.VMEM((hps, PEER_TOPK, tm), F32),
                        pltpu.VMEM((PEER_PAIRS, tm), F32), pltpu.VMEM((PEER_PAIRS, tm), F32)],
        compiler_params=pltpu.CompilerParams(
            dimension_semantics=("parallel", "arbitrary"), vmem_limit_bytes=VMEM_LIMIT_DENSE),
        name="peer_topk",
    )(qp, keys, pos)


EXPERT_WORDS = ROWS_PER_EXPERT // 2
SLOTS = PEER_PAIRS // 4
SLOT_ROWS = SLOTS * TILE_ROWS
MXU_ROWS = 256
SLOTS_PER_MXU_TILE = MXU_ROWS // TILE_ROWS
INDEX_SLOT_TOKENS = 64
PEER_TOKEN_BLOCK = 2 * INDEX_SLOT_TOKENS


class _IndexSlots:
    def __init__(self, idx_hbm, idx_ref, sem_ref):
        self.idx_hbm, self.idx_ref, self.sem_ref = idx_hbm, idx_ref, sem_ref
        self.step = pl.program_id(0)
        self.last = pl.num_programs(0) - 1

    def _copy(self, step, slot):
        row0 = pl.multiple_of((2 * step + slot) * INDEX_SLOT_TOKENS, INDEX_SLOT_TOKENS)
        return pltpu.make_async_copy(self.idx_hbm.at[pl.ds(row0, INDEX_SLOT_TOKENS)],
                                     self.idx_ref.at[slot], self.sem_ref.at[slot])

    def prime(self):
        @pl.when(self.step == 0)
        def _():
            self._copy(self.step, 0).start()
            self._copy(self.step, 1).start()

    def before_gather(self, t):
        if t % INDEX_SLOT_TOKENS == 0:
            self._copy(self.step, t // INDEX_SLOT_TOKENS).wait()

    def after_gather(self, t):
        if t % INDEX_SLOT_TOKENS == INDEX_SLOT_TOKENS - 1:
            self._copy(jnp.minimum(self.step + 1, self.last), t // INDEX_SLOT_TOKENS).start()

    def drain(self):
        @pl.when(self.step == self.last)
        def _():
            self._copy(self.step, 0).wait()
            self._copy(self.step, 1).wait()

    def __call__(self, t, j):
        return self.idx_ref[t // INDEX_SLOT_TOKENS, t % INDEX_SLOT_TOKENS, j]


def _index_scratch():
    return [pltpu.SMEM((2, INDEX_SLOT_TOKENS, PEER_PAIRS), jnp.int32), pltpu.SemaphoreType.DMA((2,))]


def _gather_slots(index, tab_ref, t, scale=None):
    index.before_gather(t)

    def expert(j):
        row0 = pl.multiple_of(index(t, j), EXPERT_WORDS)
        return tab_ref[pl.ds(row0, EXPERT_WORDS), :]

    def tile(j):
        words = jnp.concatenate([expert(j), expert(j + 1)], axis=0)
        x = pltpu.bitcast(words, BF16)
        return x if scale is None else x * scale

    blocks = []
    for k in range(SLOTS // SLOTS_PER_MXU_TILE):
        slots = [jnp.concatenate([tile(2 * i), tile(PEER_PAIRS // 2 + 2 * i)], axis=1)
                 for i in range(k * SLOTS_PER_MXU_TILE, (k + 1) * SLOTS_PER_MXU_TILE)]
        blocks.append(jnp.concatenate(slots, axis=0))
    index.after_gather(t)
    return blocks


def _peer_u_kernel(idx_hbm, h_ref, gate_ref, half_ref, tab_ref, act_ref, idx_ref, sem_ref):
    tb = h_ref.shape[0]
    ones = jnp.ones((BF16_SUBLANES, LANES), BF16)
    half_sum = half_ref[...]
    index = _IndexSlots(idx_hbm, idx_ref, sem_ref)
    index.prime()

    def gather(t):
        ht = h_ref[t].astype(BF16)
        return _gather_slots(index, tab_ref, t, scale=jnp.concatenate([ht, ht], axis=0))

    def half_sums(prod):
        z = [_dot(half_sum, block) for block in prod]
        return jnp.concatenate(z, axis=0).astype(BF16)

    def lane_sums(t, z):
        s = _dot_nt(ones, jnp.concatenate([z[:, :LANES], z[:, LANES:]], axis=0))
        act_ref[t:t + 1, :] = s[0:1, :]

    prod = gather(0)
    pending = None
    for t in range(tb):
        nxt = gather(t + 1) if t + 1 < tb else None
        z = half_sums(prod)
        if pending is not None:
            lane_sums(t - 1, pending)
        pending, prod = z, nxt
    lane_sums(tb - 1, pending)
    index.drain()
    a = act_ref[...]
    act_ref[...] = 0.5 * a * (1.0 + lax.erf(a * (2.0 ** -0.5))) * gate_ref[...]


def _peer_u(idx, h3, gate, tab, tb):
    t = idx.shape[0]
    r = jnp.arange(2 * SLOTS_PER_MXU_TILE)[:, None]
    c = jnp.arange(MXU_ROWS)[None, :]
    half_sum = ((c // TILE_ROWS == r // 2) & ((c % TILE_ROWS) // ROWS_PER_EXPERT == r % 2)).astype(BF16)
    row = pl.BlockSpec((tb, PEER_PAIRS), lambda i: (i, 0))
    return pl.pallas_call(
        _peer_u_kernel,
        grid=(t // tb,),
        in_specs=[
            pl.BlockSpec(memory_space=pl.ANY),
            pl.BlockSpec((tb, ROWS_PER_EXPERT, LANES), lambda i: (i, 0, 0)),
            row,
            pl.BlockSpec(half_sum.shape, lambda i: (0, 0)),
            pl.BlockSpec(memory_space=pltpu.VMEM),
        ],
        out_specs=row,
        out_shape=jax.ShapeDtypeStruct((t, PEER_PAIRS), F32),
        scratch_shapes=_index_scratch(),
        compiler_params=pltpu.CompilerParams(
            dimension_semantics=("arbitrary",), vmem_limit_bytes=VMEM_LIMIT_TABLE),
        name="peer_u",
    )(idx, h3, gate, half_sum, tab)


def _peer_v_kernel(idx_hbm, act_ref, exp_ref, tab_ref, y_ref, coef_ref, idx_ref, sem_ref):
    tb = act_ref.shape[0]
    n = ROWS_PER_EXPERT
    index = _IndexSlots(idx_hbm, idx_ref, sem_ref)
    index.prime()
    act = act_ref[...]
    hi = act.astype(BF16)
    lhs = jnp.concatenate([hi, (act - hi.astype(F32)).astype(BF16)], axis=0)
    lhs = jnp.broadcast_to(lhs[:, None, :], (2 * tb, n, PEER_PAIRS)).reshape(2 * tb * n, PEER_PAIRS)
    coef_ref[...] = _dot(lhs, exp_ref[...])
    shape = (n, SLOT_ROWS)
    own_row = (lax.broadcasted_iota(jnp.int32, shape, 1) % n) == lax.broadcasted_iota(jnp.int32, shape, 0)

    def consume(t, xs):
        def part(kind, side):
            r0 = (kind * tb + t) * n
            return jnp.where(own_row, coef_ref[r0:r0 + n, side * SLOT_ROWS:(side + 1) * SLOT_ROWS], 0.0)

        coef = jnp.concatenate([part(0, 0), part(0, 1), part(1, 0), part(1, 1)], axis=0).astype(BF16)
        res = _dot(coef, jnp.concatenate(xs, axis=0))
        y_ref[t] =((res[0:n, :LANES] + res[2 * n:3 * n, :LANES])
                    + (res[n:2 * n, LANES:] + res[3 * n:4 * n, LANES:]))

    xs = _gather_slots(index, tab_ref, 0)
    for t in range(tb):
        nxt = _gather_slots(index, tab_ref, t + 1) if t + 1 < tb else None
        consume(t, xs)
        xs = nxt
    index.drain()


def _peer_v(idx, act, tab, tb):
    t = idx.shape[0]
    j = jnp.arange(PEER_PAIRS)[:, None]
    c = jnp.arange(2 * SLOT_ROWS)[None, :]
    pair_of_col = ((PEER_PAIRS // 2) * (c // SLOT_ROWS) + 2 * ((c % SLOT_ROWS) // TILE_ROWS)
                   + (c % TILE_ROWS) // ROWS_PER_EXPERT)
    expand = (pair_of_col == j).astype(BF16)
    row = pl.BlockSpec((tb, PEER_PAIRS), lambda i: (i, 0))
    return pl.pallas_call(
        _peer_v_kernel,
        grid=(t // tb,),
        in_specs=[
            pl.BlockSpec(memory_space=pl.ANY),
            row,
            pl.BlockSpec(expand.shape, lambda i: (0, 0)),
            pl.BlockSpec(memory_space=pltpu.VMEM),
        ],
        out_specs=pl.BlockSpec((tb, ROWS_PER_EXPERT, LANES), lambda i: (i, 0, 0)),
        out_shape=jax.ShapeDtypeStruct((t, ROWS_PER_EXPERT, LANES), F32),
        scratch_shapes=[pltpu.VMEM((2 * tb * ROWS_PER_EXPERT, 2 * SLOT_ROWS), F32)] + _index_scratch(),
        compiler_params=pltpu.CompilerParams(
            dimension_semantics=("arbitrary",), vmem_limit_bytes=VMEM_LIMIT_TABLE),
        name="peer_v",
    )(idx, act, expand, tab)


def _ple_kernel(x1_ref, y_ref, p_ref, g_ref, wg_ref, wp_ref, o_ref):
    y = jnp.concatenate([y_ref[:, r, :] for r in range(ROWS_PER_EXPERT)], axis=1)
    x2 = x1_ref[...] + y
    hg = _rms(x2, g_ref[...])
    gate = jax.nn.sigmoid(_dot(hg.astype(BF16), wg_ref[...]))
    o_ref[...] = x2 + gate * _dot(p_ref[...].astype(BF16), wp_ref[...])


def _ple(x1, y, p2d, g, wg, wp, tm):
    t, d = x1.shape
    row = lambda w: pl.BlockSpec((tm, w), lambda i: (i, 0))
    full = lambda a: pl.BlockSpec(a.shape, lambda i: (0,) * a.ndim)
    return pl.pallas_call(
        _ple_kernel,
        grid=(t // tm,),
        in_specs=[row(d), pl.BlockSpec((tm, ROWS_PER_EXPERT, LANES), lambda i: (i, 0, 0)), row(p2d.shape[1]),
                  full(g), full(wg), full(wp)],
        out_specs=row(d),
        out_shape=jax.ShapeDtypeStruct((t, d), F32),
        compiler_params=pltpu.CompilerParams(
            dimension_semantics=("parallel",), vmem_limit_bytes=VMEM_LIMIT_DENSE),
        name="ple",
    )(x1, y, p2d, g, wg, wp)


def _expert_table_kernel(emb_ref, tab_ref):
    ne = emb_ref.shape[0]

    def bf16_bits(x):
        return lax.bitcast_convert_type(x.astype(BF16).astype(F32), jnp.uint32)

    for s in range(EXPERT_WORDS):
        lo = bf16_bits(emb_ref[:, (2 * s) * LANES:(2 * s + 1) * LANES])
        hi = bf16_bits(emb_ref[:, (2 * s + 1) * LANES:(2 * s + 2) * LANES])
        tab_ref[pl.ds(s, ne, stride=EXPERT_WORDS), :] = (lo >> 16) | hi


def _expert_table(emb):
    n, d = emb.shape
    assert d == ROWS_PER_EXPERT * LANES
    ne = 256
    return pl.pallas_call(
        _expert_table_kernel,
        grid=(n // ne,),
        in_specs=[pl.BlockSpec((ne, d), lambda i: (i, 0))],
        out_specs=pl.BlockSpec((ne * EXPERT_WORDS, LANES), lambda i: (i, 0)),
        out_shape=jax.ShapeDtypeStruct((n * EXPERT_WORDS, LANES), jnp.uint32),
        compiler_params=pltpu.CompilerParams(dimension_semantics=("parallel",)),
        name="expert_table",
    )(emb)


def _token_tiles(t):
    assert t % CHUNK == 0 and t % PEER_TOKEN_BLOCK == 0
    dense = next(m for m in (512, 256, CHUNK) if t % m == 0)
    topk = 256 if t % 256 == 0 else CHUNK
    return dense, topk, PEER_TOKEN_BLOCK


def _layer(x, p, g_mix, w_in, q_norm, k_norm, sinks, w_out, g_ffn, w_query, sub_keys, emb_u, emb_v,
           g_ple, w_gate, w_proj):
    b, s, d = x.shape
    t = b * s
    tm, tk, tb = _token_tiles(t)
    x2d = x.reshape(t, d)
    proj = _in_proj(x2d, g_mix[None, :], w_in.astype(BF16), tm)
    proj3 = proj.reshape(b, s, proj.shape[1])
    ro = _retention(proj3, _retention_tables(s))
    so = _swa(proj3, sinks, q_norm[None, :], k_norm[None, :])
    wo = w_out.astype(BF16)
    x1, h3, qp = _out_proj(x2d, ro.reshape(t, RET_WIDTH), so.reshape(t, SWA_Q_WIDTH),
                           wo[:RET_WIDTH], wo[RET_WIDTH:], g_ffn[None, :], w_query.astype(BF16), tm)
    idx, gate = _peer_topk(qp, sub_keys.astype(BF16), tk)
    act = _peer_u(idx, h3, gate, _expert_table(emb_u), tb)
    y = _peer_v(idx, act, _expert_table(emb_v), tb)
    out = _ple(x1, y, p.reshape(t, p.shape[-1]), g_ple[None, :],
               w_gate.astype(BF16), w_proj.astype(BF16), tm)
    return out.reshape(b, s, d)


def kernel(x, p, g_mix, w_in, q_norm, k_norm, sinks, w_out, g_ffn, peer_w_query, peer_sub_keys,
           peer_u, peer_v, g_ple, w_ple_gate, w_ple_proj):
    for i in range(p.shape[0]):
        x = _layer(x, p[i], g_mix[i], w_in[i], q_norm[i], k_norm[i], sinks[i], w_out[i], g_ffn[i],
                   peer_w_query[i], peer_sub_keys[i], peer_u[i], peer_v[i], g_ple[i], w_ple_gate[i],
                   w_ple_proj[i])
    return x
```

```python
import jax
import jax.numpy as jnp
from jax import lax
from jax.experimental import pallas as pl
from jax.experimental.pallas import tpu as pltpu

F32 = jnp.float32
BF16 = jnp.bfloat16

HEAD_DIM = 64
RET_HEADS = 8
SWA_Q_HEADS = 8
SWA_KV_HEADS = 2
SWA_GROUP = SWA_Q_HEADS // SWA_KV_HEADS
RET_WIDTH = RET_HEADS * HEAD_DIM
SWA_Q_WIDTH = SWA_Q_HEADS * HEAD_DIM
SWA_KV_WIDTH = SWA_KV_HEADS * HEAD_DIM
CHUNK = 128
ROPE_BASE = 10000.0
PEER_HEADS = 8
PEER_KEYS = 128
PEER_HALF = 128
PEER_TOPK = 16
PEER_PAIRS = PEER_HEADS * PEER_TOPK
EPS = 1e-6

LANES = 128
F32_SUBLANES = 8
BF16_SUBLANES = 16
VMEM_LIMIT_DENSE = 48 * 1024 * 1024
VMEM_LIMIT_TABLE = 56 * 1024 * 1024

ROWS_PER_EXPERT = 1024 // LANES
TILE_ROWS = BF16_SUBLANES
EXPERT_TABLE_BLOCK = 1024


def _rms(x, g):
    ms = jnp.mean(x * x, axis=-1, keepdims=True)
    return x * lax.rsqrt(ms + EPS) * g


def _dot(a, b):
    return jnp.dot(a, b, preferred_element_type=F32)


def _dot_nt(a, b):
    return lax.dot_general(a, b, (((1,), (1,)), ((), ())), preferred_element_type=F32)


def _dot_tn(a, b):
    return lax.dot_general(a, b, (((0,), (0,)), ((), ())), preferred_element_type=F32)


def _in_proj_kernel(x_ref, g_ref, w_ref, o_ref):
    h = _rms(x_ref[...], g_ref[...])
    o_ref[...] = _dot(h.astype(BF16), w_ref[...])


def _in_proj(x2d, g, w, tm):
    t, d = x2d.shape
    n = w.shape[1]
    return pl.pallas_call(
        _in_proj_kernel,
        grid=(t // tm,),
        in_specs=[
            pl.BlockSpec((tm, d), lambda i: (i, 0)),
            pl.BlockSpec((1, d), lambda i: (0, 0)),
            pl.BlockSpec((d, n), lambda i: (0, 0)),
        ],
        out_specs=pl.BlockSpec((tm, n), lambda i: (i, 0)),
        out_shape=jax.ShapeDtypeStruct((t, n), F32),
        compiler_params=pltpu.CompilerParams(
            dimension_semantics=("parallel",), vmem_limit_bytes=VMEM_LIMIT_DENSE),
        name="in_proj",
    )(x2d, g, w)


def _retention_kernel(cd_ref, q_ref, k_ref, v_ref, g_ref, cos_ref, sin_ref, dec_ref, wk_ref, wq_ref,
                      o_ref, state_ref):
    n = pl.program_id(1)

    @pl.when(n == 0)
    def _():
        state_ref[...] = jnp.zeros_like(state_ref)

    c = cos_ref[...]
    s = sin_ref[...]
    lane = lax.broadcasted_iota(jnp.int32, (CHUNK, RET_WIDTH), 1)
    first_half = (lane % HEAD_DIM) < (HEAD_DIM // 2)

    def rot(x):
        partner = jnp.where(first_half,
                            pltpu.roll(x, RET_WIDTH - HEAD_DIM // 2, 1),
                            pltpu.roll(x, HEAD_DIM // 2, 1))
        return x * c + partner * s

    wq = wq_ref[...]
    for bb in range(q_ref.shape[0]):
        q = rot(q_ref[bb])
        k = rot(k_ref[bb]) * (HEAD_DIM ** -0.5)
        kw = k * wk_ref[...]
        v = v_ref[bb]
        gate = g_ref[bb]
        for h in range(RET_HEADS):
            sl = slice(h * HEAD_DIM, (h + 1) * HEAD_DIM)
            qh = q[:, sl].astype(BF16)
            kh = k[:, sl].astype(BF16)
            vh = v[:, sl].astype(BF16)
            att = _dot_nt(qh, kh) * dec_ref[h]
            o = _dot(att.astype(BF16), vh)
            st = state_ref[bb, h]
            o = o + _dot(qh, st.astype(BF16)) * wq[:, sl]
            state_ref[bb, h] = st * cd_ref[h] + _dot_tn(kw[:, sl].astype(BF16), vh)
            mu = jnp.mean(o, axis=-1, keepdims=True)
            oc = o - mu
            var = jnp.mean(oc * oc, axis=-1, keepdims=True)
            on = oc * lax.rsqrt(var + EPS)
            gh = gate[:, sl]
            o_ref[bb, :, sl] = gh * jax.nn.sigmoid(gh) * on


def _retention(proj3, tables):
    b, s, _ = proj3.shape
    cd, cos_t, sin_t, dec, wk, wq = tables
    nb = 1
    col = lambda j: pl.BlockSpec((nb, CHUNK, RET_WIDTH), lambda bi, ni, j=j: (bi, ni, j))
    pos_spec = pl.BlockSpec((CHUNK, RET_WIDTH), lambda bi, ni: (ni, 0))
    const2 = pl.BlockSpec((CHUNK, RET_WIDTH), lambda bi, ni: (0, 0))
    return pl.pallas_call(
        _retention_kernel,
        grid=(b // nb, s // CHUNK),
        in_specs=[
            pl.BlockSpec(memory_space=pltpu.SMEM),
            col(0), col(1), col(2), col(3),
            pos_spec, pos_spec,
            pl.BlockSpec((RET_HEADS, CHUNK, CHUNK), lambda bi, ni: (0, 0, 0)),
            const2, const2,
        ],
        out_specs=pl.BlockSpec((nb, CHUNK, RET_WIDTH), lambda bi, ni: (bi, ni, 0)),
        out_shape=jax.ShapeDtypeStruct((b, s, RET_WIDTH), F32),
        scratch_shapes=[pltpu.VMEM((nb, RET_HEADS, HEAD_DIM, HEAD_DIM), F32)],
        compiler_params=pltpu.CompilerParams(
            dimension_semantics=("parallel", "arbitrary"), vmem_limit_bytes=VMEM_LIMIT_DENSE),
        name="retention",
    )(cd, proj3, proj3, proj3, proj3, cos_t, sin_t, dec, wk, wq)


def _retention_tables(s):
    half = HEAD_DIM // 2
    pos = jnp.arange(s, dtype=F32)
    freqs = ROPE_BASE ** (-jnp.arange(half, dtype=F32) / half)
    ang = pos[:, None] * freqs[None, :]
    c = jnp.cos(ang)
    sn = jnp.sin(ang)
    cos_t = jnp.tile(jnp.concatenate([c, c], axis=-1), (1, RET_HEADS))
    sin_t = jnp.tile(jnp.concatenate([-sn, sn], axis=-1), (1, RET_HEADS))
    gamma = 1.0 - jnp.exp2(-5.0 - jnp.arange(RET_HEADS, dtype=F32))
    log_g = jnp.log(gamma)
    idx = jnp.arange(CHUNK)
    rel = idx[:, None] - idx[None, :]
    dec = jnp.where(rel[None] >= 0,
                    jnp.exp(log_g[:, None, None] * jnp.maximum(rel, 0)[None].astype(F32)),
                    0.0)
    w_k = jnp.exp(log_g[None, :] * (CHUNK - 1 - idx)[:, None].astype(F32))
    w_q = jnp.exp(log_g[None, :] * (idx + 1)[:, None].astype(F32))
    cd = jnp.exp(log_g * CHUNK)
    wk = jnp.repeat(w_k, HEAD_DIM, axis=1)
    wq = jnp.repeat(w_q, HEAD_DIM, axis=1)
    return cd, cos_t, sin_t, dec, wk, wq


def _swa_kernel(sink_ref, q_ref, kc_ref, vc_ref, kp_ref, vp_ref, qn_ref, kn_ref, o_ref):
    n = pl.program_id(1)
    rows = SWA_GROUP * CHUNK
    row = lax.broadcasted_iota(jnp.int32, (rows, 2 * CHUNK), 0)
    ki = lax.broadcasted_iota(jnp.int32, (rows, 2 * CHUNK), 1)
    rel = row % CHUNK + CHUNK - ki
    mask = (rel >= 0) & (rel < CHUNK) & ((n > 0) | (ki >= CHUNK))
    row_head = lax.broadcasted_iota(jnp.int32, (rows, 1), 0) // CHUNK
    qn = qn_ref[...]
    kn = kn_ref[...]
    for bb in range(q_ref.shape[0]):
        q = q_ref[bb]
        for g in range(SWA_KV_HEADS):
            sl = slice(g * HEAD_DIM, (g + 1) * HEAD_DIM)
            kcat = jnp.concatenate([kp_ref[bb, :, sl], kc_ref[bb, :, sl]], axis=0)
            kcat = _rms(kcat, kn).astype(BF16)
            vcat = jnp.concatenate([vp_ref[bb, :, sl], vc_ref[bb, :, sl]], axis=0).astype(BF16)
            heads = [g * SWA_GROUP + j for j in range(SWA_GROUP)]
            qs = jnp.concatenate([_rms(q[:, h * HEAD_DIM:(h + 1) * HEAD_DIM], qn) for h in heads], axis=0)
            sink = jnp.full((rows, 1), sink_ref[heads[0]], F32)
            for j in range(1, SWA_GROUP):
                sink = jnp.where(row_head == j, sink_ref[heads[j]], sink)
            sc = _dot_nt(qs.astype(BF16), kcat) * (HEAD_DIM ** -0.5)
            sc = jnp.where(mask, sc, -jnp.inf)
            m = jnp.maximum(jnp.max(sc, axis=-1, keepdims=True), sink)
            e = jnp.exp(sc - m)
            pr = e / (jnp.sum(e, axis=-1, keepdims=True) + jnp.exp(sink - m))
            o = _dot(pr.astype(BF16), vcat)
            for j, h in enumerate(heads):
                o_ref[bb, :, h * HEAD_DIM:(h + 1) * HEAD_DIM] = o[j * CHUNK:(j + 1) * CHUNK]


def _swa(proj3, sinks, qn, kn):
    b, s, _ = proj3.shape
    q_blk = (4 * RET_WIDTH) // SWA_Q_WIDTH
    k_blk = (4 * RET_WIDTH + SWA_Q_WIDTH) // SWA_KV_WIDTH
    v_blk = k_blk + 1
    nb = 1
    cur = lambda j: pl.BlockSpec((nb, CHUNK, SWA_KV_WIDTH), lambda bi, ni, j=j: (bi, ni, j))
    prev = lambda j: pl.BlockSpec((nb, CHUNK, SWA_KV_WIDTH),
                                  lambda bi, ni, j=j: (bi, jnp.maximum(ni - 1, 0), j))
    return pl.pallas_call(
        _swa_kernel,
        grid=(b // nb, s // CHUNK),
        in_specs=[
            pl.BlockSpec(memory_space=pltpu.SMEM),
            pl.BlockSpec((nb, CHUNK, SWA_Q_WIDTH), lambda bi, ni: (bi, ni, q_blk)),
            cur(k_blk), cur(v_blk), prev(k_blk), prev(v_blk),
            pl.BlockSpec((1, HEAD_DIM), lambda bi, ni: (0, 0)),
            pl.BlockSpec((1, HEAD_DIM), lambda bi, ni: (0, 0)),
        ],
        out_specs=pl.BlockSpec((nb, CHUNK, SWA_Q_WIDTH), lambda bi, ni: (bi, ni, 0)),
        out_shape=jax.ShapeDtypeStruct((b, s, SWA_Q_WIDTH), F32),
        compiler_params=pltpu.CompilerParams(
            dimension_semantics=("parallel", "parallel"), vmem_limit_bytes=VMEM_LIMIT_DENSE),
        name="swa",
    )(sinks, proj3, proj3, proj3, proj3, proj3, qn, kn)


def _out_proj_kernel(x_ref, ro_ref, so_ref, wo1_ref, wo2_ref, g_ref, wq_ref, x1_ref, h2_ref, qp_ref):
    x1 = (x_ref[...] + _dot(ro_ref[...].astype(BF16), wo1_ref[...])
          + _dot(so_ref[...].astype(BF16), wo2_ref[...]))
    x1_ref[...] = x1
    h2 = _rms(x1, g_ref[...])
    for r in range(ROWS_PER_EXPERT):
        h2_ref[:, r, :] = h2[:, r * LANES:(r + 1) * LANES]
    qp = _dot(h2.astype(BF16), wq_ref[...])
    width = qp_ref.shape[2]
    for h in range(PEER_HEADS):
        qp_ref[h] = qp[:, h * width:(h + 1) * width]


def _out_proj(x2d, ro, so, wo1, wo2, g, wq, tm):
    t, d = x2d.shape
    nq = wq.shape[1]
    width = nq // PEER_HEADS
    row = lambda w: pl.BlockSpec((tm, w), lambda i: (i, 0))
    full = lambda a: pl.BlockSpec(a.shape, lambda i: (0,) * a.ndim)
    return pl.pallas_call(
        _out_proj_kernel,
        grid=(t // tm,),
        in_specs=[row(d), row(ro.shape[1]), row(so.shape[1]), full(wo1), full(wo2), full(g), full(wq)],
        out_specs=[row(d), pl.BlockSpec((tm, ROWS_PER_EXPERT, LANES), lambda i: (i, 0, 0)),
                   pl.BlockSpec((PEER_HEADS, tm, width), lambda i: (0, i, 0))],
        out_shape=[jax.ShapeDtypeStruct((t, d), F32), jax.ShapeDtypeStruct((t, ROWS_PER_EXPERT, LANES), F32),
                   jax.ShapeDtypeStruct((PEER_HEADS, t, width), F32)],
        compiler_params=pltpu.CompilerParams(
            dimension_semantics=("parallel",), vmem_limit_bytes=VMEM_LIMIT_DENSE),
        name="out_proj",
    )(x2d, ro, so, wo1, wo2, g, wq)


_CAND_B = (16, 8, 8, 8, 8, 8, 8, 8)
TOPK_HEADS_PER_STEP = 8


def _candidate_flat_index():
    pos = [a * PEER_TOPK + b for a, nb in enumerate(_CAND_B) for b in range(nb)]
    pos += [a * PEER_TOPK for a in range(len(_CAND_B), PEER_TOPK)]
    return pos


def _peer_topk_kernel(qp_ref, keys_ref, pos_ref, idx_ref, gate_ref,
                      s_ref, i_ref, best_ref, exp_ref, et_ref, gt_ref):
    g = pl.program_id(1)
    heads, tm = qp_ref.shape[0], qp_ref.shape[1]
    neg = -jnp.inf
    key_id = lax.broadcasted_iota(jnp.int32, (PEER_KEYS, tm), 0).astype(F32)
    pos = pos_ref[...]
    for hh in range(heads):
        for p in range(2):
            qhp = qp_ref[hh, :, p * PEER_HALF:(p + 1) * PEER_HALF].astype(BF16)
            sc = _dot_nt(keys_ref[p], qhp)
            for a in range(PEER_TOPK):
                m = jnp.max(sc, axis=0, keepdims=True)
                i = jnp.min(jnp.where(sc == m, key_id, float(PEER_KEYS)), axis=0, keepdims=True)
                sc = jnp.where(key_id == i, neg, sc)
                s_ref[hh, p, a:a + 1, :] = m
                i_ref[hh, p, a:a + 1, :] = i
    for hh in range(heads):
        s0, s1 = s_ref[hh, 0], s_ref[hh, 1]
        i0, i1 = i_ref[hh, 0] * float(PEER_KEYS), i_ref[hh, 1]
        cs_blocks, ci_blocks = [], []
        for a, nb in enumerate(_CAND_B):
            cs_blocks.append(s0[a:a + 1] + s1[0:nb])
            ci_blocks.append(i0[a:a + 1] + i1[0:nb])
        na = len(_CAND_B)
        cs_blocks.append(s0[na:] + s1[0:1])
        ci_blocks.append(i0[na:] + i1[0:1])
        cs = jnp.concatenate(cs_blocks, axis=0)
        ci = jnp.concatenate(ci_blocks, axis=0)
        for k in range(PEER_TOPK):
            m = jnp.max(cs, axis=0, keepdims=True)
            first = jnp.min(jnp.where(cs == m, pos, float(PEER_TOPK * PEER_TOPK)), axis=0, keepdims=True)
            sel = pos == first
            best_ref[hh, k:k + 1, :] = m
            exp_ref[hh, k:k + 1, :] = jnp.max(jnp.where(sel, ci, -1.0), axis=0, keepdims=True)
            cs = jnp.where(sel, neg, cs)
        best = best_ref[hh]
        ex = jnp.exp(best - best[0:1])
        gate = ex / jnp.sum(ex, axis=0, keepdims=True)
        rows = pl.ds(pl.multiple_of((g * heads + hh) * PEER_TOPK, PEER_TOPK), PEER_TOPK)
        et_ref[rows, :] = exp_ref[hh]
        gt_ref[rows, :] = gate

    @pl.when(g == pl.num_programs(1) - 1)
    def _():
        idx_ref[...] = et_ref[...].T.astype(jnp.int32)
        gate_ref[...] = gt_ref[...].T


def _peer_topk(qp, keys, tm):
    nh, t, width = qp.shape
    hps = TOPK_HEADS_PER_STEP
    pos = _candidate_flat_index()
    pos = jnp.broadcast_to(jnp.asarray(pos, F32)[:, None], (len(pos), tm))
    full = lambda a: pl.BlockSpec(a.shape, lambda i, h: (0,) * a.ndim)
    out_spec = pl.BlockSpec((tm, PEER_PAIRS), lambda i, h: (i, 0))
    return pl.pallas_call(
        _peer_topk_kernel,
        grid=(t // tm, nh // hps),
        in_specs=[pl.BlockSpec((hps, tm, width), lambda i, h: (h, i, 0)), full(keys), full(pos)],
        out_specs=[out_spec, out_spec],
        out_shape=[jax.ShapeDtypeStruct((t, PEER_PAIRS), jnp.int32),
                   jax.ShapeDtypeStruct((t, PEER_PAIRS), F32)],
        scratch_shapes=[pltpu.VMEM((hps, 2, PEER_TOPK, tm), F32), pltpu.VMEM((hps, 2, PEER_TOPK, tm), F32),
                        pltpu.VMEM((hps, PEER_TOPK, tm), F32), pltpu.VMEM((hps, PEER_TOPK, tm), F32),
                        pltpu.VMEM((PEER_PAIRS, tm), F32), pltpu.VMEM((PEER_PAIRS, tm), F32)],
        compiler_params=pltpu.CompilerParams(
            dimension_semantics=("parallel", "arbitrary"), vmem_limit_bytes=VMEM_LIMIT_DENSE),
        name="peer_topk",
    )(qp, keys, pos)


SLOTS = PEER_PAIRS // 4
SLOT_ROWS = SLOTS * TILE_ROWS
MXU_ROWS = 256
SLOTS_PER_MXU_TILE = MXU_ROWS // TILE_ROWS
INDEX_SLOT_TOKENS = 64
PEER_TOKEN_BLOCK = 2 * INDEX_SLOT_TOKENS


class _IndexSlots:
    def __init__(self, idx_hbm, idx_ref, sem_ref):
        self.idx_hbm, self.idx_ref, self.sem_ref = idx_hbm, idx_ref, sem_ref
        self.step = pl.program_id(0)
        self.last = pl.num_programs(0) - 1

    def _copy(self, step, slot):
        row0 = pl.multiple_of((2 * step + slot) * INDEX_SLOT_TOKENS, INDEX_SLOT_TOKENS)
        return pltpu.make_async_copy(self.idx_hbm.at[pl.ds(row0, INDEX_SLOT_TOKENS)],
                                     self.idx_ref.at[slot], self.sem_ref.at[slot])

    def prime(self):
        @pl.when(self.step == 0)
        def _():
            self._copy(self.step, 0).start()
            self._copy(self.step, 1).start()

    def before_gather(self, t):
        if t % INDEX_SLOT_TOKENS == 0:
            self._copy(self.step, t // INDEX_SLOT_TOKENS).wait()

    def after_gather(self, t):
        if t % INDEX_SLOT_TOKENS == INDEX_SLOT_TOKENS - 1:
            self._copy(jnp.minimum(self.step + 1, self.last), t // INDEX_SLOT_TOKENS).start()

    def drain(self):
        @pl.when(self.step == self.last)
        def _():
            self._copy(self.step, 0).wait()
            self._copy(self.step, 1).wait()

    def __call__(self, t, j):
        return self.idx_ref[t // INDEX_SLOT_TOKENS, t % INDEX_SLOT_TOKENS, j]


def _index_scratch():
    return [pltpu.SMEM((2, INDEX_SLOT_TOKENS, PEER_PAIRS), jnp.int32), pltpu.SemaphoreType.DMA((2,))]


def _gather_slots(index, tab_ref, t, scale=None):
    index.before_gather(t)

    def tile(j):
        x = jnp.concatenate([tab_ref[index(t, j)], tab_ref[index(t, j + 1)]], axis=0)
        return x if scale is None else x * scale

    blocks = []
    for k in range(SLOTS // SLOTS_PER_MXU_TILE):
        slots = [jnp.concatenate([tile(2 * i), tile(PEER_PAIRS // 2 + 2 * i)], axis=1)
                 for i in range(k * SLOTS_PER_MXU_TILE, (k + 1) * SLOTS_PER_MXU_TILE)]
        blocks.append(jnp.concatenate(slots, axis=0))
    index.after_gather(t)
    return blocks


def _peer_u_kernel(idx_hbm, h_ref, gate_ref, half_ref, tab_ref, act_ref, idx_ref, sem_ref):
    tb = h_ref.shape[0]
    ones = jnp.ones((BF16_SUBLANES, LANES), BF16)
    half_sum = half_ref[...]
    index = _IndexSlots(idx_hbm, idx_ref, sem_ref)
    index.prime()

    def gather(t):
        ht = h_ref[t].astype(BF16)
        return _gather_slots(index, tab_ref, t, scale=jnp.concatenate([ht, ht], axis=0))

    def half_sums(prod):
        z = [_dot(half_sum, block) for block in prod]
        return jnp.concatenate(z, axis=0).astype(BF16)

    def lane_sums(t, z):
        s = _dot_nt(ones, jnp.concatenate([z[:, :LANES], z[:, LANES:]], axis=0))
        act_ref[t:t + 1, :] = s[0:1, :]

    prod = gather(0)
    pending = None
    for t in range(tb):
        nxt = gather(t + 1) if t + 1 < tb else None
        z = half_sums(prod)
        if pending is not None:
            lane_sums(t - 1, pending)
        pending, prod = z, nxt
    lane_sums(tb - 1, pending)
    index.drain()
    a = act_ref[...]
    act_ref[...] = 0.5 * a * (1.0 + lax.erf(a * (2.0 ** -0.5))) * gate_ref[...]


def _peer_u(idx, h3, gate, tab, tb):
    t = idx.shape[0]
    r = jnp.arange(2 * SLOTS_PER_MXU_TILE)[:, None]
    c = jnp.arange(MXU_ROWS)[None, :]
    half_sum = ((c // TILE_ROWS == r // 2) & ((c % TILE_ROWS) // ROWS_PER_EXPERT == r % 2)).astype(BF16)
    row = pl.BlockSpec((tb, PEER_PAIRS), lambda i: (i, 0))
    return pl.pallas_call(
        _peer_u_kernel,
        grid=(t // tb,),
        in_specs=[
            pl.BlockSpec(memory_space=pl.ANY),
            pl.BlockSpec((tb, ROWS_PER_EXPERT, LANES), lambda i: (i, 0, 0)),
            row,
            pl.BlockSpec(half_sum.shape, lambda i: (0, 0)),
            pl.BlockSpec(memory_space=pltpu.VMEM),
        ],
        out_specs=row,
        out_shape=jax.ShapeDtypeStruct((t, PEER_PAIRS), F32),
        scratch_shapes=_index_scratch(),
        compiler_params=pltpu.CompilerParams(
            dimension_semantics=("arbitrary",), vmem_limit_bytes=VMEM_LIMIT_TABLE),
        name="peer_u",
    )(idx, h3, gate, half_sum, tab)


def _peer_v_kernel(idx_hbm, act_ref, exp_ref, tab_ref, y_ref, coef_ref, idx_ref, sem_ref):
    tb = act_ref.shape[0]
    n = ROWS_PER_EXPERT
    index = _IndexSlots(idx_hbm, idx_ref, sem_ref)
    index.prime()
    act = act_ref[...]
    hi = act.astype(BF16)
    lhs = jnp.concatenate([hi, (act - hi.astype(F32)).astype(BF16)], axis=0)
    lhs = jnp.broadcast_to(lhs[:, None, :], (2 * tb, n, PEER_PAIRS)).reshape(2 * tb * n, PEER_PAIRS)
    coef_ref[...] = _dot(lhs, exp_ref[...])
    shape = (n, SLOT_ROWS)
    own_row = (lax.broadcasted_iota(jnp.int32, shape, 1) % n) == lax.broadcasted_iota(jnp.int32, shape, 0)

    def consume(t, xs):
        def part(kind, side):
            r0 = (kind * tb + t) * n
            return jnp.where(own_row, coef_ref[r0:r0 + n, side * SLOT_ROWS:(side + 1) * SLOT_ROWS], 0.0)

        coef = jnp.concatenate([part(0, 0), part(0, 1), part(1, 0), part(1, 1)], axis=0).astype(BF16)
        res = _dot(coef, jnp.concatenate(xs, axis=0))
        y_ref[t] =((res[0:n, :LANES] + res[2 * n:3 * n, :LANES])
                    + (res[n:2 * n, LANES:] + res[3 * n:4 * n, LANES:]))

    xs = _gather_slots(index, tab_ref, 0)
    for t in range(tb):
        nxt = _gather_slots(index, tab_ref, t + 1) if t + 1 < tb else None
        consume(t, xs)
        xs = nxt
    index.drain()


def _peer_v(idx, act, tab, tb):
    t = idx.shape[0]
    j = jnp.arange(PEER_PAIRS)[:, None]
    c = jnp.arange(2 * SLOT_ROWS)[None, :]
    pair_of_col = ((PEER_PAIRS // 2) * (c // SLOT_ROWS) + 2 * ((c % SLOT_ROWS) // TILE_ROWS)
                   + (c % TILE_ROWS) // ROWS_PER_EXPERT)
    expand = (pair_of_col == j).astype(BF16)
    row = pl.BlockSpec((tb, PEER_PAIRS), lambda i: (i, 0))
    return pl.pallas_call(
        _peer_v_kernel,
        grid=(t // tb,),
        in_specs=[
            pl.BlockSpec(memory_space=pl.ANY),
            row,
            pl.BlockSpec(expand.shape, lambda i: (0, 0)),
            pl.BlockSpec(memory_space=pltpu.VMEM),
        ],
        out_specs=pl.BlockSpec((tb, ROWS_PER_EXPERT, LANES), lambda i: (i, 0, 0)),
        out_shape=jax.ShapeDtypeStruct((t, ROWS_PER_EXPERT, LANES), F32),
        scratch_shapes=[pltpu.VMEM((2 * tb * ROWS_PER_EXPERT, 2 * SLOT_ROWS), F32)] + _index_scratch(),
        compiler_params=pltpu.CompilerParams(
            dimension_semantics=("arbitrary",), vmem_limit_bytes=VMEM_LIMIT_TABLE),
        name="peer_v",
    )(idx, act, expand, tab)


def _ple_kernel(x1_ref, y_ref, p_ref, g_ref, wg_ref, wp_ref, o_ref):
    y = jnp.concatenate([y_ref[:, r, :] for r in range(ROWS_PER_EXPERT)], axis=1)
    x2 = x1_ref[...] + y
    hg = _rms(x2, g_ref[...])
    gate = jax.nn.sigmoid(_dot(hg.astype(BF16), wg_ref[...]))
    o_ref[...] = x2 + gate * _dot(p_ref[...].astype(BF16), wp_ref[...])


def _ple(x1, y, p2d, g, wg, wp, tm):
    t, d = x1.shape
    row = lambda w: pl.BlockSpec((tm, w), lambda i: (i, 0))
    full = lambda a: pl.BlockSpec(a.shape, lambda i: (0,) * a.ndim)
    return pl.pallas_call(
        _ple_kernel,
        grid=(t // tm,),
        in_specs=[row(d), pl.BlockSpec((tm, ROWS_PER_EXPERT, LANES), lambda i: (i, 0, 0)), row(p2d.shape[1]),
                  full(g), full(wg), full(wp)],
        out_specs=row(d),
        out_shape=jax.ShapeDtypeStruct((t, d), F32),
        compiler_params=pltpu.CompilerParams(
            dimension_semantics=("parallel",), vmem_limit_bytes=VMEM_LIMIT_DENSE),
        name="ple",
    )(x1, y, p2d, g, wg, wp)


def _expert_table_kernel(emb_ref, tab_ref):
    for r in range(ROWS_PER_EXPERT):
        tab_ref[:, r, :] = emb_ref[:, r * LANES:(r + 1) * LANES].astype(BF16)


def _expert_table(emb):
    n, d = emb.shape
    ne = EXPERT_TABLE_BLOCK
    assert d == ROWS_PER_EXPERT * LANES and n % ne == 0
    return pl.pallas_call(
        _expert_table_kernel,
        grid=(n // ne,),
        in_specs=[pl.BlockSpec((ne, d), lambda i: (i, 0))],
        out_specs=pl.BlockSpec((ne, ROWS_PER_EXPERT, LANES), lambda i: (i, 0, 0)),
        out_shape=jax.ShapeDtypeStruct((n, ROWS_PER_EXPERT, LANES), BF16),
        compiler_params=pltpu.CompilerParams(dimension_semantics=("parallel",)),
        name="expert_table",
    )(emb)


def _token_tiles(t):
    assert t % CHUNK == 0 and t % PEER_TOKEN_BLOCK == 0
    dense = next(m for m in (512, 256, CHUNK) if t % m == 0)
    topk = 256 if t % 256 == 0 else CHUNK
    return dense, topk, PEER_TOKEN_BLOCK


def _layer(x, p, g_mix, w_in, q_norm, k_norm, sinks, w_out, g_ffn, w_query, sub_keys, emb_u, emb_v,
           g_ple, w_gate, w_proj):
    b, s, d = x.shape
    t = b * s
    tm, tk, tb = _token_tiles(t)
    x2d = x.reshape(t, d)
    proj = _in_proj(x2d, g_mix[None, :], w_in.astype(BF16), tm)
    proj3 = proj.reshape(b, s, proj.shape[1])
    ro = _retention(proj3, _retention_tables(s))
    so = _swa(proj3, sinks, q_norm[None, :], k_norm[None, :])
    wo = w_out.astype(BF16)
    x1, h3, qp = _out_proj(x2d, ro.reshape(t, RET_WIDTH), so.reshape(t, SWA_Q_WIDTH),
                           wo[:RET_WIDTH], wo[RET_WIDTH:], g_ffn[None, :], w_query.astype(BF16), tm)
    idx, gate = _peer_topk(qp, sub_keys.astype(BF16), tk)
    act = _peer_u(idx, h3, gate, _expert_table(emb_u), tb)
    y = _peer_v(idx, act, _expert_table(emb_v), tb)
    out = _ple(x1, y, p.reshape(t, p.shape[-1]), g_ple[None, :],
               w_gate.astype(BF16), w_proj.astype(BF16), tm)
    return out.reshape(b, s, d)


def kernel(x, p, g_mix, w_in, q_norm, k_norm, sinks, w_out, g_ffn, peer_w_query, peer_sub_keys,
           peer_u, peer_v, g_ple, w_ple_gate, w_ple_proj):
    for i in range(p.shape[0]):
        x = _layer(x, p[i], g_mix[i], w_in[i], q_norm[i], k_norm[i], sinks[i], w_out[i], g_ffn[i],
                   peer_w_query[i], peer_sub_keys[i], peer_u[i], peer_v[i], g_ple[i], w_ple_gate[i],
                   w_ple_proj[i])
    return x
```

```python
import jax
import jax.numpy as jnp
from jax import lax
from jax.experimental import pallas as pl
from jax.experimental.pallas import tpu as pltpu

F32 = jnp.float32
BF16 = jnp.bfloat16

HEAD_DIM = 64
RET_HEADS = 8
SWA_Q_HEADS = 8
SWA_KV_HEADS = 2
SWA_GROUP = SWA_Q_HEADS // SWA_KV_HEADS
RET_WIDTH = RET_HEADS * HEAD_DIM
SWA_Q_WIDTH = SWA_Q_HEADS * HEAD_DIM
SWA_KV_WIDTH = SWA_KV_HEADS * HEAD_DIM
CHUNK = 128
ROPE_BASE = 10000.0
PEER_HEADS = 8
PEER_KEYS = 128
PEER_HALF = 128
PEER_TOPK = 16
PEER_PAIRS = PEER_HEADS * PEER_TOPK
EPS = 1e-6

LANES = 128
F32_SUBLANES = 8
BF16_SUBLANES = 16
VMEM_LIMIT_DENSE = 48 * 1024 * 1024
VMEM_LIMIT_TABLE = 56 * 1024 * 1024

ROWS_PER_EXPERT = 1024 // LANES
TILE_ROWS = BF16_SUBLANES
EXPERT_TABLE_BLOCK = 1024


def _rms(x, g):
    ms = jnp.mean(x * x, axis=-1, keepdims=True)
    return x * lax.rsqrt(ms + EPS) * g


def _dot(a, b):
    return jnp.dot(a, b, preferred_element_type=F32)


def _dot_nt(a, b):
    return lax.dot_general(a, b, (((1,), (1,)), ((), ())), preferred_element_type=F32)


def _dot_tn(a, b):
    return lax.dot_general(a, b, (((0,), (0,)), ((), ())), preferred_element_type=F32)


def _in_proj_kernel(x_ref, g_ref, w_ref, o_ref):
    h = _rms(x_ref[...], g_ref[...])
    o_ref[...] = _dot(h.astype(BF16), w_ref[...])


def _in_proj(x2d, g, w, tm):
    t, d = x2d.shape
    n = w.shape[1]
    return pl.pallas_call(
        _in_proj_kernel,
        grid=(t // tm,),
        in_specs=[
            pl.BlockSpec((tm, d), lambda i: (i, 0)),
            pl.BlockSpec((1, d), lambda i: (0, 0)),
            pl.BlockSpec((d, n), lambda i: (0, 0)),
        ],
        out_specs=pl.BlockSpec((tm, n), lambda i: (i, 0)),
        out_shape=jax.ShapeDtypeStruct((t, n), F32),
        compiler_params=pltpu.CompilerParams(
            dimension_semantics=("parallel",), vmem_limit_bytes=VMEM_LIMIT_DENSE),
        name="in_proj",
    )(x2d, g, w)


def _retention_kernel(cd_ref, q_ref, k_ref, v_ref, g_ref, cos_ref, sin_ref, dec_ref, wk_ref, wq_ref,
                      o_ref, state_ref):
    n = pl.program_id(1)

    @pl.when(n == 0)
    def _():
        state_ref[...] = jnp.zeros_like(state_ref)

    c = cos_ref[...]
    s = sin_ref[...]
    lane = lax.broadcasted_iota(jnp.int32, (CHUNK, RET_WIDTH), 1)
    first_half = (lane % HEAD_DIM) < (HEAD_DIM // 2)

    def rot(x):
        partner = jnp.where(first_half,
                            pltpu.roll(x, RET_WIDTH - HEAD_DIM // 2, 1),
                            pltpu.roll(x, HEAD_DIM // 2, 1))
        return x * c + partner * s

    wq = wq_ref[...]
    for bb in range(q_ref.shape[0]):
        q = rot(q_ref[bb])
        k = rot(k_ref[bb]) * (HEAD_DIM ** -0.5)
        kw = k * wk_ref[...]
        v = v_ref[bb]
        gate = g_ref[bb]
        for h in range(RET_HEADS):
            sl = slice(h * HEAD_DIM, (h + 1) * HEAD_DIM)
            qh = q[:, sl].astype(BF16)
            kh = k[:, sl].astype(BF16)
            vh = v[:, sl].astype(BF16)
            att = _dot_nt(qh, kh) * dec_ref[h]
            o = _dot(att.astype(BF16), vh)
            st = state_ref[bb, h]
            o = o + _dot(qh, st.astype(BF16)) * wq[:, sl]
            state_ref[bb, h] = st * cd_ref[h] + _dot_tn(kw[:, sl].astype(BF16), vh)
            mu = jnp.mean(o, axis=-1, keepdims=True)
            oc = o - mu
            var = jnp.mean(oc * oc, axis=-1, keepdims=True)
            on = oc * lax.rsqrt(var + EPS)
            gh = gate[:, sl]
            o_ref[bb, :, sl] = gh * jax.nn.sigmoid(gh) * on


def _retention_tables(s):
    half = HEAD_DIM // 2
    pos = jnp.arange(s, dtype=F32)
    freqs = ROPE_BASE ** (-jnp.arange(half, dtype=F32) / half)
    ang = pos[:, None] * freqs[None, :]
    c = jnp.cos(ang)
    sn = jnp.sin(ang)
    cos_t = jnp.tile(jnp.concatenate([c, c], axis=-1), (1, RET_HEADS))
    sin_t = jnp.tile(jnp.concatenate([-sn, sn], axis=-1), (1, RET_HEADS))
    gamma = 1.0 - jnp.exp2(-5.0 - jnp.arange(RET_HEADS, dtype=F32))
    log_g = jnp.log(gamma)
    idx = jnp.arange(CHUNK)
    rel = idx[:, None] - idx[None, :]
    dec = jnp.where(rel[None] >= 0,
                    jnp.exp(log_g[:, None, None] * jnp.maximum(rel, 0)[None].astype(F32)),
                    0.0)
    w_k = jnp.exp(log_g[None, :] * (CHUNK - 1 - idx)[:, None].astype(F32))
    w_q = jnp.exp(log_g[None, :] * (idx + 1)[:, None].astype(F32))
    cd = jnp.exp(log_g * CHUNK)
    wk = jnp.repeat(w_k, HEAD_DIM, axis=1)
    wq = jnp.repeat(w_q, HEAD_DIM, axis=1)
    return cd, cos_t, sin_t, dec, wk, wq


def _swa_kernel(sink_ref, q_ref, kc_ref, vc_ref, kp_ref, vp_ref, qn_ref, kn_ref, o_ref):
    n = pl.program_id(1)
    rows = SWA_GROUP * CHUNK
    row = lax.broadcasted_iota(jnp.int32, (rows, 2 * CHUNK), 0)
    ki = lax.broadcasted_iota(jnp.int32, (rows, 2 * CHUNK), 1)
    rel = row % CHUNK + CHUNK - ki
    mask = (rel >= 0) & (rel < CHUNK) & ((n > 0) | (ki >= CHUNK))
    row_head = lax.broadcasted_iota(jnp.int32, (rows, 1), 0) // CHUNK
    qn = qn_ref[...]
    kn = kn_ref[...]
    for bb in range(q_ref.shape[0]):
        q = q_ref[bb]
        for g in range(SWA_KV_HEADS):
            sl = slice(g * HEAD_DIM, (g + 1) * HEAD_DIM)
            kcat = jnp.concatenate([kp_ref[bb, :, sl], kc_ref[bb, :, sl]], axis=0)
            kcat = _rms(kcat, kn).astype(BF16)
            vcat = jnp.concatenate([vp_ref[bb, :, sl], vc_ref[bb, :, sl]], axis=0).astype(BF16)
            heads = [g * SWA_GROUP + j for j in range(SWA_GROUP)]
            qs = jnp.concatenate([_rms(q[:, h * HEAD_DIM:(h + 1) * HEAD_DIM], qn) for h in heads], axis=0)
            sink = jnp.full((rows, 1), sink_ref[heads[0]], F32)
            for j in range(1, SWA_GROUP):
                sink = jnp.where(row_head == j, sink_ref[heads[j]], sink)
            sc = _dot_nt(qs.astype(BF16), kcat) * (HEAD_DIM ** -0.5)
            sc = jnp.where(mask, sc, -jnp.inf)
            m = jnp.maximum(jnp.max(sc, axis=-1, keepdims=True), sink)
            e = jnp.exp(sc - m)
            pr = e / (jnp.sum(e, axis=-1, keepdims=True) + jnp.exp(sink - m))
            o = _dot(pr.astype(BF16), vcat)
            for j, h in enumerate(heads):
                o_ref[bb, :, h * HEAD_DIM:(h + 1) * HEAD_DIM] = o[j * CHUNK:(j + 1) * CHUNK]


def _mixer_kernel(cd_ref, rq_ref, rk_ref, rv_ref, rg_ref, cos_ref, sin_ref, dec_ref, wk_ref, wq_ref,
                  sink_ref, sq_ref, kc_ref, vc_ref, kp_ref, vp_ref, qn_ref, kn_ref,
                  ro_ref, so_ref, state_ref):
    _retention_kernel(cd_ref, rq_ref, rk_ref, rv_ref, rg_ref, cos_ref, sin_ref, dec_ref, wk_ref, wq_ref,
                      ro_ref, state_ref)
    _swa_kernel(sink_ref, sq_ref, kc_ref, vc_ref, kp_ref, vp_ref, qn_ref, kn_ref, so_ref)


def _mixer(proj3, tables, sinks, qn, kn):
    b, s, _ = proj3.shape
    cd, cos_t, sin_t, dec, wk, wq = tables
    nb = 1
    col = lambda j: pl.BlockSpec((nb, CHUNK, RET_WIDTH), lambda bi, ni, j=j: (bi, ni, j))
    pos_spec = pl.BlockSpec((CHUNK, RET_WIDTH), lambda bi, ni: (ni, 0))
    const2 = pl.BlockSpec((CHUNK, RET_WIDTH), lambda bi, ni: (0, 0))
    q_blk = (4 * RET_WIDTH) // SWA_Q_WIDTH
    k_blk = (4 * RET_WIDTH + SWA_Q_WIDTH) // SWA_KV_WIDTH
    v_blk = k_blk + 1
    cur = lambda j: pl.BlockSpec((nb, CHUNK, SWA_KV_WIDTH), lambda bi, ni, j=j: (bi, ni, j))
    prev = lambda j: pl.BlockSpec((nb, CHUNK, SWA_KV_WIDTH),
                                  lambda bi, ni, j=j: (bi, jnp.maximum(ni - 1, 0), j))
    out = pl.BlockSpec((nb, CHUNK, RET_WIDTH), lambda bi, ni: (bi, ni, 0))
    return pl.pallas_call(
        _mixer_kernel,
        grid=(b // nb, s // CHUNK),
        in_specs=[
            pl.BlockSpec(memory_space=pltpu.SMEM),
            col(0), col(1), col(2), col(3),
            pos_spec, pos_spec,
            pl.BlockSpec((RET_HEADS, CHUNK, CHUNK), lambda bi, ni: (0, 0, 0)),
            const2, const2,
            pl.BlockSpec(memory_space=pltpu.SMEM),
            pl.BlockSpec((nb, CHUNK, SWA_Q_WIDTH), lambda bi, ni: (bi, ni, q_blk)),
            cur(k_blk), cur(v_blk), prev(k_blk), prev(v_blk),
            pl.BlockSpec((1, HEAD_DIM), lambda bi, ni: (0, 0)),
            pl.BlockSpec((1, HEAD_DIM), lambda bi, ni: (0, 0)),
        ],
        out_specs=[out, out],
        out_shape=[jax.ShapeDtypeStruct((b, s, RET_WIDTH), F32), jax.ShapeDtypeStruct((b, s, SWA_Q_WIDTH), F32)],
        scratch_shapes=[pltpu.VMEM((nb, RET_HEADS, HEAD_DIM, HEAD_DIM), F32)],
        compiler_params=pltpu.CompilerParams(
            dimension_semantics=("parallel", "arbitrary"), vmem_limit_bytes=VMEM_LIMIT_DENSE),
        name="mixer",
    )(cd, proj3, proj3, proj3, proj3, cos_t, sin_t, dec, wk, wq,
      sinks, proj3, proj3, proj3, proj3, proj3, qn, kn)


def _out_proj_kernel(x_ref, ro_ref, so_ref, wo1_ref, wo2_ref, g_ref, wq_ref, x1_ref, h2_ref, qp_ref):
    x1 = (x_ref[...] + _dot(ro_ref[...].astype(BF16), wo1_ref[...])
          + _dot(so_ref[...].astype(BF16), wo2_ref[...]))
    x1_ref[...] = x1
    h2 = _rms(x1, g_ref[...])
    for r in range(ROWS_PER_EXPERT):
        h2_ref[:, r, :] = h2[:, r * LANES:(r + 1) * LANES]
    qp = _dot(h2.astype(BF16), wq_ref[...])
    width = qp_ref.shape[2]
    for h in range(PEER_HEADS):
        qp_ref[h] = qp[:, h * width:(h + 1) * width]


def _out_proj(x2d, ro, so, wo1, wo2, g, wq, tm):
    t, d = x2d.shape
    nq = wq.shape[1]
    width = nq // PEER_HEADS
    row = lambda w: pl.BlockSpec((tm, w), lambda i: (i, 0))
    full = lambda a: pl.BlockSpec(a.shape, lambda i: (0,) * a.ndim)
    return pl.pallas_call(
        _out_proj_kernel,
        grid=(t // tm,),
        in_specs=[row(d), row(ro.shape[1]), row(so.shape[1]), full(wo1), full(wo2), full(g), full(wq)],
        out_specs=[row(d), pl.BlockSpec((tm, ROWS_PER_EXPERT, LANES), lambda i: (i, 0, 0)),
                   pl.BlockSpec((PEER_HEADS, tm, width), lambda i: (0, i, 0))],
        out_shape=[jax.ShapeDtypeStruct((t, d), F32), jax.ShapeDtypeStruct((t, ROWS_PER_EXPERT, LANES), F32),
                   jax.ShapeDtypeStruct((PEER_HEADS, t, width), F32)],
        compiler_params=pltpu.CompilerParams(
            dimension_semantics=("parallel",), vmem_limit_bytes=VMEM_LIMIT_DENSE),
        name="out_proj",
    )(x2d, ro, so, wo1, wo2, g, wq)


_CAND_B = (16, 8, 8, 8, 8, 8, 8, 8)
TOPK_HEADS_PER_STEP = 8


def _candidate_flat_index():
    pos = [a * PEER_TOPK + b for a, nb in enumerate(_CAND_B) for b in range(nb)]
    pos += [a * PEER_TOPK for a in range(len(_CAND_B), PEER_TOPK)]
    return pos


def _peer_topk_kernel(qp_ref, keys_ref, pos_ref, idx_ref, gate_ref,
                      s_ref, i_ref, best_ref, exp_ref, et_ref, gt_ref):
    g = pl.program_id(1)
    heads, tm = qp_ref.shape[0], qp_ref.shape[1]
    neg = -jnp.inf
    key_id = lax.broadcasted_iota(jnp.int32, (PEER_KEYS, tm), 0).astype(F32)
    pos = pos_ref[...]
    for hh in range(heads):
        for p in range(2):
            qhp = qp_ref[hh, :, p * PEER_HALF:(p + 1) * PEER_HALF].astype(BF16)
            sc = _dot_nt(keys_ref[p], qhp)
            for a in range(PEER_TOPK):
                m = jnp.max(sc, axis=0, keepdims=True)
                i = jnp.min(jnp.where(sc == m, key_id, float(PEER_KEYS)), axis=0, keepdims=True)
                sc = jnp.where(key_id == i, neg, sc)
                s_ref[hh, p, a:a + 1, :] = m
                i_ref[hh, p, a:a + 1, :] = i
    for hh in range(heads):
        s0, s1 = s_ref[hh, 0], s_ref[hh, 1]
        i0, i1 = i_ref[hh, 0] * float(PEER_KEYS), i_ref[hh, 1]
        cs_blocks, ci_blocks = [], []
        for a, nb in enumerate(_CAND_B):
            cs_blocks.append(s0[a:a + 1] + s1[0:nb])
            ci_blocks.append(i0[a:a + 1] + i1[0:nb])
        na = len(_CAND_B)
        cs_blocks.append(s0[na:] + s1[0:1])
        ci_blocks.append(i0[na:] + i1[0:1])
        cs = jnp.concatenate(cs_blocks, axis=0)
        ci = jnp.concatenate(ci_blocks, axis=0)
        for k in range(PEER_TOPK):
            m = jnp.max(cs, axis=0, keepdims=True)
            first = jnp.min(jnp.where(cs == m, pos, float(PEER_TOPK * PEER_TOPK)), axis=0, keepdims=True)
            sel = pos == first
            best_ref[hh, k:k + 1, :] = m
            exp_ref[hh, k:k + 1, :] = jnp.max(jnp.where(sel, ci, -1.0), axis=0, keepdims=True)
            cs = jnp.where(sel, neg, cs)
        best = best_ref[hh]
        ex = jnp.exp(best - best[0:1])
        gate = ex / jnp.sum(ex, axis=0, keepdims=True)
        rows = pl.ds(pl.multiple_of((g * heads + hh) * PEER_TOPK, PEER_TOPK), PEER_TOPK)
        et_ref[rows, :] = exp_ref[hh]
        gt_ref[rows, :] = gate

    @pl.when(g == pl.num_programs(1) - 1)
    def _():
        idx_ref[...] = et_ref[...].T.astype(jnp.int32)
        gate_ref[...] = gt_ref[...].T


def _peer_topk(qp, keys, tm):
    nh, t, width = qp.shape
    hps = TOPK_HEADS_PER_STEP
    pos = _candidate_flat_index()
    pos = jnp.broadcast_to(jnp.asarray(pos, F32)[:, None], (len(pos), tm))
    full = lambda a: pl.BlockSpec(a.shape, lambda i, h: (0,) * a.ndim)
    out_spec = pl.BlockSpec((tm, PEER_PAIRS), lambda i, h: (i, 0))
    return pl.pallas_call(
        _peer_topk_kernel,
        grid=(t // tm, nh // hps),
        in_specs=[pl.BlockSpec((hps, tm, width), lambda i, h: (h, i, 0)), full(keys), full(pos)],
        out_specs=[out_spec, out_spec],
        out_shape=[jax.ShapeDtypeStruct((t, PEER_PAIRS), jnp.int32),
                   jax.ShapeDtypeStruct((t, PEER_PAIRS), F32)],
        scratch_shapes=[pltpu.VMEM((hps, 2, PEER_TOPK, tm), F32), pltpu.VMEM((hps, 2, PEER_TOPK, tm), F32),
                        pltpu.VMEM((hps, PEER_TOPK, tm), F32), pltpu.VMEM((hps, PEER_TOPK, tm), F32),
                        pltpu.VMEM((PEER_PAIRS, tm), F32), pltpu.VMEM((PEER_PAIRS, tm), F32)],
        compiler_params=pltpu.CompilerParams(
            dimension_semantics=("parallel", "arbitrary"), vmem_limit_bytes=VMEM_LIMIT_DENSE),
        name="peer_topk",
    )(qp, keys, pos)


SLOTS = PEER_PAIRS // 4
SLOT_ROWS = SLOTS * TILE_ROWS
MXU_ROWS = 256
SLOTS_PER_MXU_TILE = MXU_ROWS // TILE_ROWS
INDEX_SLOT_TOKENS = 64
PEER_TOKEN_BLOCK = 2 * INDEX_SLOT_TOKENS


class _IndexSlots:
    def __init__(self, idx_hbm, idx_ref, sem_ref):
        self.idx_hbm, self.idx_ref, self.sem_ref = idx_hbm, idx_ref, sem_ref
        self.step = pl.program_id(0)
        self.last = pl.num_programs(0) - 1

    def _copy(self, step, slot):
        row0 = pl.multiple_of((2 * step + slot) * INDEX_SLOT_TOKENS, INDEX_SLOT_TOKENS)
        return pltpu.make_async_copy(self.idx_hbm.at[pl.ds(row0, INDEX_SLOT_TOKENS)],
                                     self.idx_ref.at[slot], self.sem_ref.at[slot])

    def prime(self):
        @pl.when(self.step == 0)
        def _():
            self._copy(self.step, 0).start()
            self._copy(self.step, 1).start()

    def before_gather(self, t):
        if t % INDEX_SLOT_TOKENS == 0:
            self._copy(self.step, t // INDEX_SLOT_TOKENS).wait()

    def after_gather(self, t):
        if t % INDEX_SLOT_TOKENS == INDEX_SLOT_TOKENS - 1:
            self._copy(jnp.minimum(self.step + 1, self.last), t // INDEX_SLOT_TOKENS).start()

    def drain(self):
        @pl.when(self.step == self.last)
        def _():
            self._copy(self.step, 0).wait()
            self._copy(self.step, 1).wait()

    def __call__(self, t, j):
        return self.idx_ref[t // INDEX_SLOT_TOKENS, t % INDEX_SLOT_TOKENS, j]


def _index_scratch():
    return [pltpu.SMEM((2, INDEX_SLOT_TOKENS, PEER_PAIRS), jnp.int32), pltpu.SemaphoreType.DMA((2,))]


def _gather_slots(index, tab_ref, t, scale=None):
    index.before_gather(t)

    def tile(j):
        x = jnp.concatenate([tab_ref[index(t, j)], tab_ref[index(t, j + 1)]], axis=0)
        return x if scale is None else x * scale

    blocks = []
    for k in range(SLOTS // SLOTS_PER_MXU_TILE):
        slots = [jnp.concatenate([tile(2 * i), tile(PEER_PAIRS // 2 + 2 * i)], axis=1)
                 for i in range(k * SLOTS_PER_MXU_TILE, (k + 1) * SLOTS_PER_MXU_TILE)]
        blocks.append(jnp.concatenate(slots, axis=0))
    index.after_gather(t)
    return blocks


def _peer_u_kernel(idx_hbm, h_ref, gate_ref, half_ref, tab_ref, act_ref, idx_ref, sem_ref):
    tb = h_ref.shape[0]
    ones = jnp.ones((BF16_SUBLANES, LANES), BF16)
    half_sum = half_ref[...]
    index = _IndexSlots(idx_hbm, idx_ref, sem_ref)
    index.prime()

    def gather(t):
        ht = h_ref[t].astype(BF16)
        return _gather_slots(index, tab_ref, t, scale=jnp.concatenate([ht, ht], axis=0))

    def half_sums(prod):
        z = [_dot(half_sum, block) for block in prod]
        return jnp.concatenate(z, axis=0).astype(BF16)

    def lane_sums(t, z):
        s = _dot_nt(ones, jnp.concatenate([z[:, :LANES], z[:, LANES:]], axis=0))
        act_ref[t:t + 1, :] = s[0:1, :]

    prod = gather(0)
    pending = None
    for t in range(tb):
        nxt = gather(t + 1) if t + 1 < tb else None
        z = half_sums(prod)
        if pending is not None:
            lane_sums(t - 1, pending)
        pending, prod = z, nxt
    lane_sums(tb - 1, pending)
    index.drain()
    a = act_ref[...]
    act_ref[...] = 0.5 * a * (1.0 + lax.erf(a * (2.0 ** -0.5))) * gate_ref[...]


def _peer_u(idx, h3, gate, tab, tb):
    t = idx.shape[0]
    r = jnp.arange(2 * SLOTS_PER_MXU_TILE)[:, None]
    c = jnp.arange(MXU_ROWS)[None, :]
    half_sum = ((c // TILE_ROWS == r // 2) & ((c % TILE_ROWS) // ROWS_PER_EXPERT == r % 2)).astype(BF16)
    row = pl.BlockSpec((tb, PEER_PAIRS), lambda i: (i, 0))
    return pl.pallas_call(
        _peer_u_kernel,
        grid=(t // tb,),
        in_specs=[
            pl.BlockSpec(memory_space=pl.ANY),
            pl.BlockSpec((tb, ROWS_PER_EXPERT, LANES), lambda i: (i, 0, 0)),
            row,
            pl.BlockSpec(half_sum.shape, lambda i: (0, 0)),
            pl.BlockSpec(memory_space=pltpu.VMEM),
        ],
        out_specs=row,
        out_shape=jax.ShapeDtypeStruct((t, PEER_PAIRS), F32),
        scratch_shapes=_index_scratch(),
        compiler_params=pltpu.CompilerParams(
            dimension_semantics=("arbitrary",), vmem_limit_bytes=VMEM_LIMIT_TABLE),
        name="peer_u",
    )(idx, h3, gate, half_sum, tab)


def _peer_v_kernel(idx_hbm, act_ref, exp_ref, tab_ref, y_ref, coef_ref, idx_ref, sem_ref):
    tb = act_ref.shape[0]
    n = ROWS_PER_EXPERT
    index = _IndexSlots(idx_hbm, idx_ref, sem_ref)
    index.prime()
    act = act_ref[...]
    hi = act.astype(BF16)
    lhs = jnp.concatenate([hi, (act - hi.astype(F32)).astype(BF16)], axis=0)
    lhs = jnp.broadcast_to(lhs[:, None, :], (2 * tb, n, PEER_PAIRS)).reshape(2 * tb * n, PEER_PAIRS)
    coef_ref[...] = _dot(lhs, exp_ref[...])
    shape = (n, SLOT_ROWS)
    own_row = (lax.broadcasted_iota(jnp.int32, shape, 1) % n) == lax.broadcasted_iota(jnp.int32, shape, 0)

    def consume(t, xs):
        def part(kind, side):
            r0 = (kind * tb + t) * n
            return jnp.where(own_row, coef_ref[r0:r0 + n, side * SLOT_ROWS:(side + 1) * SLOT_ROWS], 0.0)

        coef = jnp.concatenate([part(0, 0), part(0, 1), part(1, 0), part(1, 1)], axis=0).astype(BF16)
        res = _dot(coef, jnp.concatenate(xs, axis=0))
        y_ref[t] =((res[0:n, :LANES] + res[2 * n:3 * n, :LANES])
                    + (res[n:2 * n, LANES:] + res[3 * n:4 * n, LANES:]))

    xs = _gather_slots(index, tab_ref, 0)
    for t in range(tb):
        nxt = _gather_slots(index, tab_ref, t + 1) if t + 1 < tb else None
        consume(t, xs)
        xs = nxt
    index.drain()


def _peer_v(idx, act, tab, tb):
    t = idx.shape[0]
    j = jnp.arange(PEER_PAIRS)[:, None]
    c = jnp.arange(2 * SLOT_ROWS)[None, :]
    pair_of_col = ((PEER_PAIRS // 2) * (c // SLOT_ROWS) + 2 * ((c % SLOT_ROWS) // TILE_ROWS)
                   + (c % TILE_ROWS) // ROWS_PER_EXPERT)
    expand = (pair_of_col == j).astype(BF16)
    row = pl.BlockSpec((tb, PEER_PAIRS), lambda i: (i, 0))
    return pl.pallas_call(
        _peer_v_kernel,
        grid=(t // tb,),
        in_specs=[
            pl.BlockSpec(memory_space=pl.ANY),
            row,
            pl.BlockSpec(expand.shape, lambda i: (0, 0)),
            pl.BlockSpec(memory_space=pltpu.VMEM),
        ],
        out_specs=pl.BlockSpec((tb, ROWS_PER_EXPERT, LANES), lambda i: (i, 0, 0)),
        out_shape=jax.ShapeDtypeStruct((t, ROWS_PER_EXPERT, LANES), F32),
        scratch_shapes=[pltpu.VMEM((2 * tb * ROWS_PER_EXPERT, 2 * SLOT_ROWS), F32)] + _index_scratch(),
        compiler_params=pltpu.CompilerParams(
            dimension_semantics=("arbitrary",), vmem_limit_bytes=VMEM_LIMIT_TABLE),
        name="peer_v",
    )(idx, act, expand, tab)


def _ple_kernel(x1_ref, y_ref, p_ref, g_ref, wg_ref, wp_ref, o_ref):
    y = jnp.concatenate([y_ref[:, r, :] for r in range(ROWS_PER_EXPERT)], axis=1)
    x2 = x1_ref[...] + y
    hg = _rms(x2, g_ref[...])
    gate = jax.nn.sigmoid(_dot(hg.astype(BF16), wg_ref[...]))
    o_ref[...] = x2 + gate * _dot(p_ref[...].astype(BF16), wp_ref[...])


def _ple(x1, y, p2d, g, wg, wp, tm):
    t, d = x1.shape
    row = lambda w: pl.BlockSpec((tm, w), lambda i: (i, 0))
    full = lambda a: pl.BlockSpec(a.shape, lambda i: (0,) * a.ndim)
    return pl.pallas_call(
        _ple_kernel,
        grid=(t // tm,),
        in_specs=[row(d), pl.BlockSpec((tm, ROWS_PER_EXPERT, LANES), lambda i: (i, 0, 0)), row(p2d.shape[1]),
                  full(g), full(wg), full(wp)],
        out_specs=row(d),
        out_shape=jax.ShapeDtypeStruct((t, d), F32),
        compiler_params=pltpu.CompilerParams(
            dimension_semantics=("parallel",), vmem_limit_bytes=VMEM_LIMIT_DENSE),
        name="ple",
    )(x1, y, p2d, g, wg, wp)


def _expert_table_kernel(emb_ref, tab_ref):
    for r in range(ROWS_PER_EXPERT):
        tab_ref[:, r, :] = emb_ref[:, r * LANES:(r + 1) * LANES].astype(BF16)


def _expert_table(emb):
    n, d = emb.shape
    ne = EXPERT_TABLE_BLOCK
    assert d == ROWS_PER_EXPERT * LANES and n % ne == 0
    return pl.pallas_call(
        _expert_table_kernel,
        grid=(n // ne,),
        in_specs=[pl.BlockSpec((ne, d), lambda i: (i, 0))],
        out_specs=pl.BlockSpec((ne, ROWS_PER_EXPERT, LANES), lambda i: (i, 0, 0)),
        out_shape=jax.ShapeDtypeStruct((n, ROWS_PER_EXPERT, LANES), BF16),
        compiler_params=pltpu.CompilerParams(dimension_semantics=("parallel",)),
        name="expert_table",
    )(emb)


def _token_tiles(t):
    assert t % CHUNK == 0 and t % PEER_TOKEN_BLOCK == 0
    dense = next(m for m in (512, 256, CHUNK) if t % m == 0)
    topk = 256 if t % 256 == 0 else CHUNK
    return dense, topk, PEER_TOKEN_BLOCK


def _layer(x, p, g_mix, w_in, q_norm, k_norm, sinks, w_out, g_ffn, w_query, sub_keys, emb_u, emb_v,
           g_ple, w_gate, w_proj):
    b, s, d = x.shape
    t = b * s
    tm, tk, tb = _token_tiles(t)
    x2d = x.reshape(t, d)
    proj = _in_proj(x2d, g_mix[None, :], w_in.astype(BF16), tm)
    proj3 = proj.reshape(b, s, proj.shape[1])
    ro, so = _mixer(proj3, _retention_tables(s), sinks, q_norm[None, :], k_norm[None, :])
    wo = w_out.astype(BF16)
    x1, h3, qp = _out_proj(x2d, ro.reshape(t, RET_WIDTH), so.reshape(t, SWA_Q_WIDTH),
                           wo[:RET_WIDTH], wo[RET_WIDTH:], g_ffn[None, :], w_query.astype(BF16), tm)
    idx, gate = _peer_topk(qp, sub_keys.astype(BF16), tk)
    act = _peer_u(idx, h3, gate, _expert_table(emb_u), tb)
    y = _peer_v(idx, act, _expert_table(emb_v), tb)
    out = _ple(x1, y, p.reshape(t, p.shape[-1]), g_ple[None, :],
               w_gate.astype(BF16), w_proj.astype(BF16), tm)
    return out.reshape(b, s, d)


def kernel(x, p, g_mix, w_in, q_norm, k_norm, sinks, w_out, g_ffn, peer_w_query, peer_sub_keys,
           peer_u, peer_v, g_ple, w_ple_gate, w_ple_proj):
    for i in range(p.shape[0]):
        x = _layer(x, p[i], g_mix[i], w_in[i], q_norm[i], k_norm[i], sinks[i], w_out[i], g_ffn[i],
                   peer_w_query[i], peer_sub_keys[i], peer_u[i], peer_v[i], g_ple[i], w_ple_gate[i],
                   w_ple_proj[i])
    return x
```
